```python
import math
import jax, jax.numpy as jnp
from jax import lax
import numpy as np

D_MODEL = 2048
BATCH = 8
SEQ = 4096
DEPTH = 1

HEAD_DIM = 128
CONV_WIDTH = D_MODEL // 2
CONV_GROUPS = CONV_WIDTH // HEAD_DIM
DIL_PAIRS = ((128, 1), (512, 4), (2048, 16))
N_DIL = len(DIL_PAIRS)
HEADS_PER_DIL = (D_MODEL // 2) // HEAD_DIM
N_PAT_HEADS = N_DIL * HEADS_PER_DIL
ATTN_OUT_WIDTH = HEADS_PER_DIL * HEAD_DIM
MIX_WIDTH = CONV_WIDTH + ATTN_OUT_WIDTH
QKV_WIDTH = N_PAT_HEADS * HEAD_DIM
PROJ_WIDTH = 3 * CONV_WIDTH + 3 * QKV_WIDTH
SPLITS = [CONV_WIDTH, 2 * CONV_WIDTH, 3 * CONV_WIDTH,
          3 * CONV_WIDTH + QKV_WIDTH, 3 * CONV_WIDTH + 2 * QKV_WIDTH]
D_FF = 256 * (-(-(8 * D_MODEL) // (3 * 256)))
N_BUCKETS = 32
MAX_DISTANCE = 2048
BLK = 128
LN_EPS = 1e-5
ALPHA = (2 * DEPTH) ** 0.25
BETA = (8 * DEPTH) ** -0.25

kernel_name = "hybrid_conv_dilated_attn_deepnorm_layer"


def _layernorm(x, g, b):
    xf = x.astype(jnp.float32)
    mu = jnp.mean(xf, axis=-1, keepdims=True)
    var = jnp.mean(jnp.square(xf - mu), axis=-1, keepdims=True)
    return ((xf - mu) * lax.rsqrt(var + LN_EPS) * g.astype(jnp.float32) + b.astype(jnp.float32)).astype(x.dtype)


def _causal_conv3(x, w):
    xp = jnp.pad(x, ((0, 0), (2, 0), (0, 0)))
    return w[0] * xp[:, :-2] + w[1] * xp[:, 1:-1] + w[2] * xp[:, 2:]


def _t5_bucket(dist):
    max_exact = N_BUCKETS // 2
    n = np.maximum(dist, 1).astype(np.float32)
    large = max_exact + (np.log(n / max_exact) / math.log(MAX_DISTANCE / max_exact)
                         * (N_BUCKETS - max_exact)).astype(np.int32)
    large = np.minimum(large, N_BUCKETS - 1)
    return np.where(dist < max_exact, dist, large).astype(np.int32)


def _dilated_window_attention(q, k, v, bias_table, window, dilation):
    b, s, h, dh = q.shape
    n_steps = window // dilation
    L = s // dilation
    nb = -(-L // BLK)
    lp = nb * BLK

    def to_blocks(t):
        t = t.reshape(b, L, dilation, h, dh).transpose(0, 2, 3, 1, 4)
        t = jnp.pad(t, ((0, 0), (0, 0), (0, 0), (0, lp - L), (0, 0)))
        return t.reshape(b, dilation, h, nb, BLK, dh)

    def with_prev(t):
        prev = jnp.pad(t, ((0, 0), (0, 0), (0, 0), (1, 0), (0, 0), (0, 0)))[:, :, :, :-1]
        return jnp.concatenate([prev, t], axis=4)

    qb = to_blocks(q)
    kk = with_prev(to_blocks(k))
    vv = with_prev(to_blocks(v))

    qi = np.arange(BLK)[:, None]
    ki = np.arange(2 * BLK)[None, :]
    steps = qi + BLK - ki
    in_window = (steps >= 0) & (steps <= n_steps)
    bucket = _t5_bucket(np.clip(steps, 0, None) * dilation)
    bias = jnp.take(bias_table, jnp.asarray(bucket), axis=0)
    bias = bias.transpose(2, 0, 1).astype(jnp.float32)
    key_exists = (np.arange(nb)[:, None, None] * BLK - BLK + ki[None]) >= 0
    valid = jnp.asarray(in_window[None] & key_exists)

    scores = jnp.einsum('bdhnqc,bdhnkc->bdhnqk', qb, kk).astype(jnp.float32) * (dh ** -0.5)
    scores = scores + bias[:, None]
    scores = jnp.where(valid, scores, jnp.finfo(jnp.float32).min)
    m = jnp.max(scores, axis=-1, keepdims=True)
    p = jnp.exp(scores - m)
    denom = jnp.sum(p, axis=-1, keepdims=True)
    o = jnp.einsum('bdhnqk,bdhnkc->bdhnqc', p, vv.astype(jnp.float32)) / denom
    lse = (m + jnp.log(denom))

    def from_blocks(t):
        t = t.reshape(b, dilation, h, lp, t.shape[-1])[:, :, :, :L]
        return t.transpose(0, 3, 1, 2, 4).reshape(b, s, h, t.shape[-1])

    return from_blocks(o), from_blocks(lse)[..., 0]


def _token_mixer(x, w_in, conv_w, w_out, rel_bias):
    b, s, _ = x.shape
    proj = x @ w_in
    u, gate_b, gate_c, q, k, v = jnp.split(proj, SPLITS, axis=-1)
    y_conv = gate_b * _causal_conv3(gate_c * u, conv_w)
    q = q.reshape(b, s, N_DIL, HEADS_PER_DIL, HEAD_DIM)
    k = k.reshape(b, s, N_DIL, HEADS_PER_DIL, HEAD_DIM)
    v = v.reshape(b, s, N_DIL, HEADS_PER_DIL, HEAD_DIM)
    outs, lses = [], []
    for g, (window, dilation) in enumerate(DIL_PAIRS):
        o_g, lse_g = _dilated_window_attention(
            q[:, :, g], k[:, :, g], v[:, :, g],
            rel_bias[:, g * HEADS_PER_DIL:(g + 1) * HEADS_PER_DIL], window, dilation)
        outs.append(o_g)
        lses.append(lse_g)
    o_all = jnp.stack(outs, axis=0)
    wts = jax.nn.softmax(jnp.stack(lses, axis=0), axis=0)
    y_attn = jnp.sum(wts[..., None] * o_all, axis=0).reshape(b, s, ATTN_OUT_WIDTH).astype(x.dtype)
    return jnp.concatenate([y_conv, y_attn], axis=-1) @ w_out


def _conv_ffn(h, w_up, conv_w, conv_b, w_down):
    up = _causal_conv3(h @ w_up, conv_w) + conv_b
    a, g = jnp.split(up, 2, axis=-1)
    return (jax.nn.silu(g) * a) @ w_down


def setup_inputs(seed: int = 0) -> dict:
    key = jax.random.key(seed)
    ks = jax.random.split(key, 13)
    f32 = jnp.float32
    col_scale = jnp.concatenate([
        jnp.full((CONV_WIDTH,), BETA, f32),
        jnp.ones((2 * CONV_WIDTH + 2 * QKV_WIDTH,), f32),
        jnp.full((QKV_WIDTH,), BETA, f32)])
    return {
        "x": jax.random.normal(ks[0], (BATCH, SEQ, D_MODEL), f32),
        "w_in": jax.random.normal(ks[1], (DEPTH, D_MODEL, PROJ_WIDTH), f32) * (D_MODEL ** -0.5) * col_scale,
        "conv_mix_w": jax.random.normal(ks[2], (DEPTH, 3, CONV_WIDTH), f32) * (3 ** -0.5),
        "w_out": jax.random.normal(ks[3], (DEPTH, MIX_WIDTH, D_MODEL), f32) * (MIX_WIDTH ** -0.5) * BETA,
        "ln1_g": 1.0 + 0.02 * jax.random.normal(ks[4], (DEPTH, D_MODEL), f32),
        "ln1_b": 0.02 * jax.random.normal(ks[5], (DEPTH, D_MODEL), f32),
        "w_up": jax.random.normal(ks[6], (DEPTH, D_MODEL, 2 * D_FF), f32) * (D_MODEL ** -0.5) * BETA,
        "ffn_conv_w": jax.random.normal(ks[7], (DEPTH, 3, 2 * D_FF), f32) * (3 ** -0.5),
        "ffn_conv_b": 0.02 * jax.random.normal(ks[8], (DEPTH, 2 * D_FF), f32),
        "w_down": jax.random.normal(ks[9], (DEPTH, D_FF, D_MODEL), f32) * (D_FF ** -0.5) * BETA,
        "ln2_g": 1.0 + 0.02 * jax.random.normal(ks[10], (DEPTH, D_MODEL), f32),
        "ln2_b": 0.02 * jax.random.normal(ks[11], (DEPTH, D_MODEL), f32),
        "rel_bias": 0.2 * jax.random.normal(ks[12], (N_BUCKETS, N_PAT_HEADS), f32),
    }


def reference(x, w_in, conv_mix_w, w_out, ln1_g, ln1_b, w_up, ffn_conv_w, ffn_conv_b,
              w_down, ln2_g, ln2_b, rel_bias):
    for layer in range(DEPTH):
        mix = _token_mixer(x, w_in[layer], conv_mix_w[layer], w_out[layer], rel_bias)
        x = _layernorm(ALPHA * x + mix, ln1_g[layer], ln1_b[layer])
        ffn = _conv_ffn(x, w_up[layer], ffn_conv_w[layer], ffn_conv_b[layer], w_down[layer])
        x = _layernorm(ALPHA * x + ffn, ln2_g[layer], ln2_b[layer])
    return x
```

```python
import functools
import math

import jax
import jax.numpy as jnp
import numpy as np
from jax.experimental import pallas as pl
from jax.experimental.pallas import tpu as pltpu

HEAD_DIM = 128
DIL_PAIRS = ((128, 1), (512, 4), (2048, 16))
N_DIL = len(DIL_PAIRS)
BLK = 128
N_BUCKETS = 32
MAX_DISTANCE = 2048
LN_EPS = 1e-5
LANES = 128
SUBLANES = 8
VMEM_LIMIT_BYTES = 56 * 1024 * 1024
NEG = float(np.finfo(np.float32).min)

_F32 = jnp.float32
_BF16 = jnp.bfloat16


def _params(n_axes):
    return pltpu.CompilerParams(dimension_semantics=("arbitrary",) * n_axes,
                                vmem_limit_bytes=VMEM_LIMIT_BYTES)


def _pick(total, candidates):
    for c in candidates:
        if total % c == 0:
            return c
    raise ValueError(f"no tile in {candidates} divides {total}")


def _dot(a, b):
    return jnp.dot(a, b, preferred_element_type=_F32)


def _shift_rows(p, k, prev):
    rolled = pltpu.roll(p, k, axis=0)
    head_prev = pltpu.roll(prev, k, axis=0)
    row = jax.lax.broadcasted_iota(jnp.int32, head_prev.shape, 0)
    head = jnp.where(row < k, head_prev, rolled[:SUBLANES])
    return jnp.concatenate([head, rolled[SUBLANES:]], axis=0)


def _causal_conv3(p, prev, w):
    return w[0:1] * _shift_rows(p, 2, prev) + w[1:2] * _shift_rows(p, 1, prev) + w[2:3] * p


def _layernorm(z, g, b):
    mu = jnp.mean(z, axis=-1, keepdims=True)
    zc = z - mu
    var = jnp.mean(zc * zc, axis=-1, keepdims=True)
    return zc * jax.lax.rsqrt(var + LN_EPS) * g + b


def _conv_mix_kernel(x_ref, wu_ref, wb_ref, wc_ref, cw_ref, y_ref, xb_ref, carry_ref, *, seq_tiles):
    i = pl.program_id(0)
    j = pl.program_id(1)

    @pl.when(j == 0)
    def _():
        xb_ref[...] = x_ref[...].astype(_BF16)

    xb = xb_ref[...]
    p = _dot(xb, wc_ref[...]) * _dot(xb, wu_ref[...])
    prev = jnp.where(i % seq_tiles == 0, 0.0, carry_ref[j])
    carry_ref[j] = p[p.shape[0] - SUBLANES:]
    y = _dot(xb, wb_ref[...]) * _causal_conv3(p, prev, cw_ref[...])
    y_ref[...] = y.astype(y_ref.dtype)


def _conv_mix(x2, w_in_b, conv_w, seq, conv_width):
    t, d = x2.shape
    tm = _pick(seq, (512, 256, 128))
    tc = _pick(conv_width, (512, 256, 128))
    nj = conv_width // tc
    return pl.pallas_call(
        functools.partial(_conv_mix_kernel, seq_tiles=seq // tm),
        grid=(t // tm, nj),
        in_specs=[
            pl.BlockSpec((tm, d), lambda i, j: (i, 0)),
            pl.BlockSpec((d, tc), lambda i, j: (0, j)),
            pl.BlockSpec((d, tc), lambda i, j: (0, nj + j)),
            pl.BlockSpec((d, tc), lambda i, j: (0, 2 * nj + j)),
            pl.BlockSpec((3, tc), lambda i, j: (0, j)),
        ],
        out_specs=pl.BlockSpec((tm, tc), lambda i, j: (i, j)),
        out_shape=jax.ShapeDtypeStruct((t, conv_width), _BF16),
        scratch_shapes=[pltpu.VMEM((tm, d), _BF16), pltpu.VMEM((nj, SUBLANES, tc), _F32)],
        compiler_params=_params(2),
        name="conv_mix",
    )(x2, w_in_b, w_in_b, w_in_b, conv_w)


def _qkv_kernel(x_ref, w_ref, o_ref, xb_ref, *, n_q_blocks, scale):
    j = pl.program_id(1)

    @pl.when(j == 0)
    def _():
        xb_ref[...] = x_ref[...].astype(_BF16)

    acc = _dot(xb_ref[...], w_ref[...])
    acc = acc * jnp.where(j < n_q_blocks, scale, 1.0)
    o_ref[...] = acc.astype(o_ref.dtype)


def _qkv_proj(x2, w_in_b, seq, col0, qkv_width):
    t, d = x2.shape
    tm = _pick(seq, (1024, 512, 256, 128))
    tn = _pick(math.gcd(qkv_width, col0), (512, 256, 128))
    off = col0 // tn
    return pl.pallas_call(
        functools.partial(_qkv_kernel, n_q_blocks=qkv_width // tn, scale=HEAD_DIM ** -0.5),
        grid=(t // tm, 3 * qkv_width // tn),
        in_specs=[
            pl.BlockSpec((tm, d), lambda i, j: (i, 0)),
            pl.BlockSpec((d, tn), lambda i, j: (0, off + j)),
        ],
        out_specs=pl.BlockSpec((tm, tn), lambda i, j: (i, j)),
        out_shape=jax.ShapeDtypeStruct((t, 3 * qkv_width), _BF16),
        scratch_shapes=[pltpu.VMEM((tm, d), _BF16)],
        compiler_params=_params(2),
        name="qkv_proj",
    )(x2, w_in_b)


def _t5_bucket(dist):
    max_exact = N_BUCKETS // 2
    n = np.maximum(dist, 1).astype(np.float32)
    large = max_exact + (np.log(n / max_exact) / math.log(MAX_DISTANCE / max_exact)
                         * (N_BUCKETS - max_exact)).astype(np.int32)
    large = np.minimum(large, N_BUCKETS - 1)
    return np.where(dist < max_exact, dist, large).astype(np.int32)


def _attn_kernel(q_ref, kc_ref, kp_ref, vc_ref, vp_ref, bias_ref, o_ref, lse_ref, *, n_heads, n_sub):
    n = pl.program_id(2)
    row = jax.lax.broadcasted_iota(jnp.int32, (BLK, BLK), 0)
    col = jax.lax.broadcasted_iota(jnp.int32, (BLK, BLK), 1)
    in_cur = col <= row
    in_prev = col >= row
    in_prev_first = in_prev & (n > 0)
    nt = (((1,), (1,)), ((), ()))
    for s in range(n_sub):
        rows = slice(s * BLK, (s + 1) * BLK)
        prows = slice((s - 1) * BLK, s * BLK)
        lse_all = jnp.zeros((BLK, LANES), _F32)
        for h in range(n_heads):
            cols = slice(h * HEAD_DIM, (h + 1) * HEAD_DIM)
            q = q_ref[rows, cols]
            kc = kc_ref[rows, cols]
            vc = vc_ref[rows, cols]
            if s == 0:
                kp, vp, mask_p = kp_ref[:, cols], vp_ref[:, cols], in_prev_first
            else:
                kp, vp, mask_p = kc_ref[prows, cols], vc_ref[prows, cols], in_prev
            sc = jax.lax.dot_general(q, kc, nt, preferred_element_type=_F32)
            sp = jax.lax.dot_general(q, kp, nt, preferred_element_type=_F32)
            sc = jnp.where(in_cur, sc + bias_ref[h, :, BLK:], NEG)
            sp = jnp.where(mask_p, sp + bias_ref[h, :, :BLK], NEG)
            m = jnp.max(jnp.maximum(sc, sp), axis=-1, keepdims=True)
            pc = jnp.exp(sc - m)
            pp = jnp.exp(sp - m)
            den = jnp.sum(pc + pp, axis=-1, keepdims=True)
            o = _dot(pc.astype(_BF16), vc) + _dot(pp.astype(_BF16), vp)
            o_ref[rows, cols] = (o / den).astype(o_ref.dtype)
            lse_all = jnp.where(col == h, m + jnp.log(den), lse_all)
        lse_ref[rows, :] = lse_all


def _attention_group(qkv, rel_bias_g, batch, seq, g, qkv_width, window, dilation):
    n_heads = rel_bias_g.shape[1]
    gw = n_heads * HEAD_DIM
    d = dilation
    sub_len = seq // d
    assert window // d == BLK and sub_len % BLK == 0
    qb = _pick(sub_len, (512, 256, 128))
    n_sub = qb // BLK
    parts = 3 * qkv_width // gw

    qi = np.arange(BLK)[:, None]
    ki = np.arange(2 * BLK)[None, :]
    steps = qi + BLK - ki
    bucket = _t5_bucket(np.clip(steps, 0, None) * d)
    bias = jnp.take(rel_bias_g, jnp.asarray(bucket), axis=0).transpose(2, 0, 1).astype(_F32)

    qkv_v = qkv.reshape(batch, sub_len, d * 3 * qkv_width)
    q_blk = qkv_width // gw
    cur = lambda part: (lambda b, r, n: (b, n, r * parts + part * q_blk + g))
    prev = lambda part: (lambda b, r, n: (b, jnp.maximum(n * n_sub - 1, 0), r * parts + part * q_blk + g))
    o, lse = pl.pallas_call(
        functools.partial(_attn_kernel, n_heads=n_heads, n_sub=n_sub),
        grid=(batch, d, sub_len // qb),
        in_specs=[
            pl.BlockSpec((None, qb, gw), cur(0)),
            pl.BlockSpec((None, qb, gw), cur(1)),
            pl.BlockSpec((None, BLK, gw), prev(1)),
            pl.BlockSpec((None, qb, gw), cur(2)),
            pl.BlockSpec((None, BLK, gw), prev(2)),
            pl.BlockSpec((n_heads, BLK, 2 * BLK), lambda b, r, n: (0, 0, 0)),
        ],
        out_specs=[
            pl.BlockSpec((None, qb, gw), lambda b, r, n: (b, n, r)),
            pl.BlockSpec((None, qb, LANES), lambda b, r, n: (b, n, r)),
        ],
        out_shape=[
            jax.ShapeDtypeStruct((batch, sub_len, d * gw), _BF16),
            jax.ShapeDtypeStruct((batch, sub_len, d * LANES), _F32),
        ],
        compiler_params=_params(3),
        name=f"attn_d{d}",
    )(qkv_v, qkv_v, qkv_v, qkv_v, qkv_v, bias)
    return o.reshape(batch * seq, gw), lse.reshape(batch * seq, LANES)


def _mix_out_kernel(x_ref, yc_ref, o0_ref, o1_ref, o2_ref, l0_ref, l1_ref, l2_ref, w_ref, g_ref, b_ref,
                    h_ref, *, n_heads, alpha):
    o_refs = (o0_ref, o1_ref, o2_ref)
    lses = [r[...] for r in (l0_ref, l1_ref, l2_ref)]
    m = jnp.maximum(jnp.maximum(lses[0], lses[1]), lses[2])
    es = [jnp.exp(l - m) for l in lses]
    inv = 1.0 / (es[0] + es[1] + es[2])
    wts = [e * inv for e in es]
    ya = []
    for h in range(n_heads):
        cols = slice(h * HEAD_DIM, (h + 1) * HEAD_DIM)
        ya.append(sum(wts[g][:, h:h + 1] * o_refs[g][:, cols].astype(_F32) for g in range(N_DIL)))
    y_attn = jnp.concatenate(ya, axis=-1).astype(_BF16)
    cw = yc_ref.shape[1]
    mix = _dot(yc_ref[...], w_ref[:cw, :]) + _dot(y_attn, w_ref[cw:, :])
    h_ref[...] = _layernorm(alpha * x_ref[...] + mix, g_ref[...], b_ref[...])


def _mix_out(x2, y_conv, outs, lses, w_out_b, ln_g, ln_b, seq, alpha):
    t, d = x2.shape
    cw = y_conv.shape[1]
    aw = outs[0].shape[1]
    tm = _pick(seq, (512, 256, 128))
    row = lambda w: pl.BlockSpec((tm, w), lambda i: (i, 0))
    whole = lambda a: pl.BlockSpec(a.shape, lambda i: (0,) * a.ndim)
    return pl.pallas_call(
        functools.partial(_mix_out_kernel, n_heads=aw // HEAD_DIM, alpha=alpha),
        grid=(t // tm,),
        in_specs=[row(d), row(cw), row(aw), row(aw), row(aw), row(LANES), row(LANES), row(LANES),
                  whole(w_out_b), whole(ln_g), whole(ln_b)],
        out_specs=row(d),
        out_shape=jax.ShapeDtypeStruct((t, d), _F32),
        compiler_params=_params(1),
        name="mix_out",
    )(x2, y_conv, *outs, *lses, w_out_b, ln_g, ln_b)


def _ffn_kernel(h_ref, wa_ref, wg_ref, cwa_ref, cwg_ref, cba_ref, cbg_ref, wd_ref, g_ref, b_ref,
                o_ref, hb_ref, carry_a_ref, carry_g_ref, *, seq_tiles, alpha):
    i = pl.program_id(0)
    j = pl.program_id(1)

    @pl.when(j == 0)
    def _():
        hb_ref[...] = h_ref[...].astype(_BF16)

    hb = hb_ref[...]
    first = i % seq_tiles == 0

    def up_half(w_ref, cw_ref, cb_ref, carry_ref):
        up = _dot(hb, w_ref[...])
        prev = jnp.where(first, 0.0, carry_ref[j])
        carry_ref[j] = up[up.shape[0] - SUBLANES:]
        return _causal_conv3(up, prev, cw_ref[...]) + cb_ref[...]

    a = up_half(wa_ref, cwa_ref, cba_ref, carry_a_ref)
    gate = up_half(wg_ref, cwg_ref, cbg_ref, carry_g_ref)
    act = (gate * (1.0 / (1.0 + jnp.exp(-gate))) * a).astype(_BF16)
    part = _dot(act, wd_ref[...])

    @pl.when(j == 0)
    def _():
        o_ref[...] = part

    @pl.when(j > 0)
    def _():
        o_ref[...] += part

    @pl.when(j == pl.num_programs(1) - 1)
    def _():
        o_ref[...] = _layernorm(alpha * h_ref[...] + o_ref[...], g_ref[...], b_ref[...])


def _conv_ffn(h, w_up_b, conv_w, conv_b, w_down_b, ln_g, ln_b, seq, alpha):
    t, d = h.shape
    d_ff = w_down_b.shape[0]
    tm = _pick(seq, (512, 256, 128))
    tc = _pick(d_ff, (512, 256, 128))
    nj = d_ff // tc
    whole = lambda a: pl.BlockSpec(a.shape, lambda i, j: (0,) * a.ndim)
    return pl.pallas_call(
        functools.partial(_ffn_kernel, seq_tiles=seq // tm, alpha=alpha),
        grid=(t // tm, nj),
        in_specs=[
            pl.BlockSpec((tm, d), lambda i, j: (i, 0)),
            pl.BlockSpec((d, tc), lambda i, j: (0, j)),
            pl.BlockSpec((d, tc), lambda i, j: (0, nj + j)),
            pl.BlockSpec((3, tc), lambda i, j: (0, j)),
            pl.BlockSpec((3, tc), lambda i, j: (0, nj + j)),
            pl.BlockSpec((1, tc), lambda i, j: (0, j)),
            pl.BlockSpec((1, tc), lambda i, j: (0, nj + j)),
            pl.BlockSpec((tc, d), lambda i, j: (j, 0)),
            whole(ln_g), whole(ln_b),
        ],
        out_specs=pl.BlockSpec((tm, d), lambda i, j: (i, 0)),
        out_shape=jax.ShapeDtypeStruct((t, d), _F32),
        scratch_shapes=[pltpu.VMEM((tm, d), _BF16),
                        pltpu.VMEM((nj, SUBLANES, tc), _F32),
                        pltpu.VMEM((nj, SUBLANES, tc), _F32)],
        compiler_params=_params(2),
        name="conv_ffn",
    )(h, w_up_b, w_up_b, conv_w, conv_w, conv_b, conv_b, w_down_b, ln_g, ln_b)


def kernel(x, w_in, conv_mix_w, w_out, ln1_g, ln1_b, w_up, ffn_conv_w, ffn_conv_b, w_down, ln2_g, ln2_b, rel_bias):
    batch, seq, d_model = x.shape
    depth = w_in.shape[0]
    conv_width = conv_mix_w.shape[-1]
    qkv_width = (w_in.shape[-1] - 3 * conv_width) // 3
    n_heads = qkv_width // (N_DIL * HEAD_DIM)
    alpha = (2 * depth) ** 0.25
    h = x.reshape(batch * seq, d_model)
    for layer in range(depth):
        w_in_b = w_in[layer].astype(_BF16)
        y_conv = _conv_mix(h, w_in_b, conv_mix_w[layer], seq, conv_width)
        qkv = _qkv_proj(h, w_in_b, seq, 3 * conv_width, qkv_width)
        outs, lses = [], []
        for g, (window, dilation) in enumerate(DIL_PAIRS):
            o_g, lse_g = _attention_group(qkv, rel_bias[:, g * n_heads:(g + 1) * n_heads], batch, seq, g,
                                          qkv_width, window, dilation)
            outs.append(o_g)
            lses.append(lse_g)
        h = _mix_out(h, y_conv, outs, lses, w_out[layer].astype(_BF16),
                     ln1_g[layer][None], ln1_b[layer][None], seq, alpha)
        h = _conv_ffn(h, w_up[layer].astype(_BF16), ffn_conv_w[layer], ffn_conv_b[layer][None],
                      w_down[layer].astype(_BF16), ln2_g[layer][None], ln2_b[layer][None], seq, alpha)
    return h.reshape(batch, seq, d_model)
```

```python
import functools
import math

import jax
import jax.numpy as jnp
import numpy as np
from jax.experimental import pallas as pl
from jax.experimental.pallas import tpu as pltpu

HEAD_DIM = 128
DIL_PAIRS = ((128, 1), (512, 4), (2048, 16))
N_DIL = len(DIL_PAIRS)
BLK = 128
N_BUCKETS = 32
MAX_DISTANCE = 2048
LN_EPS = 1e-5
LANES = 128
SUBLANES = 8
VMEM_LIMIT_BYTES = 56 * 1024 * 1024
NEG = float(np.finfo(np.float32).min)

_F32 = jnp.float32
_BF16 = jnp.bfloat16


def _params(n_axes):
    return pltpu.CompilerParams(dimension_semantics=("arbitrary",) * n_axes,
                                vmem_limit_bytes=VMEM_LIMIT_BYTES)


def _pick(total, candidates):
    for c in candidates:
        if total % c == 0:
            return c
    raise ValueError(f"no tile in {candidates} divides {total}")


def _dot(a, b):
    return jnp.dot(a, b, preferred_element_type=_F32)


def _shift_rows(p, k, prev):
    rolled = pltpu.roll(p, k, axis=0)
    head_prev = pltpu.roll(prev, k, axis=0)
    row = jax.lax.broadcasted_iota(jnp.int32, head_prev.shape, 0)
    head = jnp.where(row < k, head_prev, rolled[:SUBLANES])
    return jnp.concatenate([head, rolled[SUBLANES:]], axis=0)


def _causal_conv3(p, prev, w):
    return w[0:1] * _shift_rows(p, 2, prev) + w[1:2] * _shift_rows(p, 1, prev) + w[2:3] * p


def _layernorm(z, g, b):
    mu = jnp.mean(z, axis=-1, keepdims=True)
    zc = z - mu
    var = jnp.mean(zc * zc, axis=-1, keepdims=True)
    return zc * jax.lax.rsqrt(var + LN_EPS) * g + b


def _x_prep_kernel(x_ref, nat_ref, *rest, dilations):
    perm_refs, stage_ref = rest[:-1], rest[-1]
    tm, d_model = x_ref.shape
    for c in range(d_model // LANES):
        cols = slice(c * LANES, (c + 1) * LANES)
        v = x_ref[:, cols]
        nat_ref[:, cols] = v.astype(_BF16)
        stage_ref[c] = v
        for d, ref in zip(dilations, perm_refs):
            n = tm // d
            for r in range(d):
                ref[r, :, cols] = stage_ref[c, pl.ds(r, n, stride=d), :].astype(_BF16)


def _x_prep(x2, batch, seq):
    t, dm = x2.shape
    dilations = tuple(d for _, d in DIL_PAIRS if d > 1)
    tm = _pick(seq, (512, 256))
    seq_tiles = seq // tm
    outs = pl.pallas_call(
        functools.partial(_x_prep_kernel, dilations=dilations),
        grid=(t // tm,),
        in_specs=[pl.BlockSpec((tm, dm), lambda i: (i, 0))],
        out_specs=[pl.BlockSpec((tm, dm), lambda i: (i, 0))] + [
            pl.BlockSpec((None, d, tm // d, dm), lambda i: (i // seq_tiles, 0, i % seq_tiles, 0))
            for d in dilations],
        out_shape=[jax.ShapeDtypeStruct((t, dm), _BF16)] + [
            jax.ShapeDtypeStruct((batch, d, seq // d, dm), _BF16) for d in dilations],
        scratch_shapes=[pltpu.VMEM((dm // LANES, tm, LANES), _F32)],
        compiler_params=_params(1),
        name="x_prep",
    )(x2)
    xs = {1: outs[0]}
    for d, o in zip(dilations, outs[1:]):
        xs[d] = o.reshape(t, dm)
    return xs


def _conv_mix_kernel(xb_ref, wu_ref, wb_ref, wc_ref, cw_ref, y_ref, carry_ref, *, seq_tiles):
    i = pl.program_id(0)
    j = pl.program_id(1)
    xb = xb_ref[...]
    p = _dot(xb, wc_ref[...]) * _dot(xb, wu_ref[...])
    prev = jnp.where(i % seq_tiles == 0, 0.0, carry_ref[j])
    carry_ref[j] = p[p.shape[0] - SUBLANES:]
    y = _dot(xb, wb_ref[...]) * _causal_conv3(p, prev, cw_ref[...])
    y_ref[...] = y.astype(y_ref.dtype)


def _conv_mix(xb, w_in_b, conv_w, seq, conv_width):
    t, d = xb.shape
    tm = _pick(seq, (512, 256, 128))
    tc = _pick(conv_width, (512, 256, 128))
    nj = conv_width // tc
    return pl.pallas_call(
        functools.partial(_conv_mix_kernel, seq_tiles=seq // tm),
        grid=(t // tm, nj),
        in_specs=[
            pl.BlockSpec((tm, d), lambda i, j: (i, 0)),
            pl.BlockSpec((d, tc), lambda i, j: (0, j)),
            pl.BlockSpec((d, tc), lambda i, j: (0, nj + j)),
            pl.BlockSpec((d, tc), lambda i, j: (0, 2 * nj + j)),
            pl.BlockSpec((3, tc), lambda i, j: (0, j)),
        ],
        out_specs=pl.BlockSpec((tm, tc), lambda i, j: (i, j)),
        out_shape=jax.ShapeDtypeStruct((t, conv_width), _BF16),
        scratch_shapes=[pltpu.VMEM((nj, SUBLANES, tc), _F32)],
        compiler_params=_params(2),
        name="conv_mix",
    )(xb, w_in_b, w_in_b, w_in_b, conv_w)


def _qkv_kernel(xb_ref, w_ref, o_ref, *, n_q_blocks, scale):
    j = pl.program_id(1)
    acc = _dot(xb_ref[...], w_ref[...])
    acc = acc * jnp.where(j < n_q_blocks, scale, 1.0)
    o_ref[...] = acc.astype(o_ref.dtype)


def _qkv_proj(xb, w_in_b, seq, col0, qkv_width, g, gw):
    t, d = xb.shape
    tm = _pick(seq, (1024, 512, 256, 128))
    tn = _pick(math.gcd(math.gcd(gw, col0), qkv_width), (512, 256, 128))
    cpp = gw // tn
    col_block = lambda j: (col0 + g * gw) // tn + (j // cpp) * (qkv_width // tn) + j % cpp
    return pl.pallas_call(
        functools.partial(_qkv_kernel, n_q_blocks=cpp, scale=HEAD_DIM ** -0.5),
        grid=(t // tm, 3 * cpp),
        in_specs=[
            pl.BlockSpec((tm, d), lambda i, j: (i, 0)),
            pl.BlockSpec((d, tn), lambda i, j: (0, col_block(j))),
        ],
        out_specs=pl.BlockSpec((tm, tn), lambda i, j: (i, j)),
        out_shape=jax.ShapeDtypeStruct((t, 3 * gw), _BF16),
        compiler_params=_params(2),
        name=f"qkv_proj_g{g}",
    )(xb, w_in_b)


def _t5_bucket(dist):
    max_exact = N_BUCKETS // 2
    n = np.maximum(dist, 1).astype(np.float32)
    large = max_exact + (np.log(n / max_exact) / math.log(MAX_DISTANCE / max_exact)
                         * (N_BUCKETS - max_exact)).astype(np.int32)
    large = np.minimum(large, N_BUCKETS - 1)
    return np.where(dist < max_exact, dist, large).astype(np.int32)


def _attn_kernel(bucket_ref, table_ref, q_ref, kc_ref, kp_ref, vc_ref, vp_ref, o_ref, lse_ref, bias_ref,
                 *, n_heads, n_sub, buckets):
    n = pl.program_id(1)

    @pl.when((pl.program_id(0) == 0) & (n == 0))
    def _():
        bucket = bucket_ref[...]
        for h in range(n_heads):
            acc = jnp.zeros(bucket.shape, _F32)
            for b in buckets:
                acc = jnp.where(bucket == b, table_ref[b, h], acc)
            bias_ref[h] = acc

    row = jax.lax.broadcasted_iota(jnp.int32, (BLK, 2 * BLK), 0)
    col = jax.lax.broadcasted_iota(jnp.int32, (BLK, 2 * BLK), 1)
    steps = row + BLK - col
    valid = (steps >= 0) & (steps <= BLK)
    valid_first = valid & ((col >= BLK) | (n > 0))
    lane = jax.lax.broadcasted_iota(jnp.int32, (BLK, LANES), 1)
    head_cols = [slice(h * HEAD_DIM, (h + 1) * HEAD_DIM) for h in range(n_heads)]
    for s in range(n_sub):
        rows = slice(s * BLK, (s + 1) * BLK)
        krows = slice((s - 1) * BLK, (s + 1) * BLK)
        q3 = jnp.stack([q_ref[rows, c] for c in head_cols])
        if s == 0:
            k3 = jnp.stack([jnp.concatenate([kp_ref[:, c], kc_ref[rows, c]], axis=0) for c in head_cols])
            v3 = jnp.stack([jnp.concatenate([vp_ref[:, c], vc_ref[rows, c]], axis=0) for c in head_cols])
        else:
            k3 = jnp.stack([kc_ref[krows, c] for c in head_cols])
            v3 = jnp.stack([vc_ref[krows, c] for c in head_cols])
        sc = jnp.einsum("hqd,hkd->hqk", q3, k3, preferred_element_type=_F32)
        sc = jnp.where((valid_first if s == 0 else valid)[None], sc + bias_ref[...], NEG)
        m = jnp.max(sc, axis=-1, keepdims=True)
        p = jnp.exp(sc - m)
        den = jnp.sum(p, axis=-1, keepdims=True)
        o = jnp.einsum("hqk,hkd->hqd", p.astype(_BF16), v3, preferred_element_type=_F32)
        o = o * (1.0 / den)
        lse = m + jnp.log(den)
        lse_all = jnp.zeros((BLK, LANES), _F32)
        for h in range(n_heads):
            o_ref[rows, head_cols[h]] = o[h].astype(o_ref.dtype)
            lse_all = jnp.where(lane == h, lse[h], lse_all)
        lse_ref[rows, :] = lse_all


def _attention_group(qkv_g, table_g, batch, seq, window, dilation):
    n_heads = table_g.shape[1]
    gw = n_heads * HEAD_DIM
    d = dilation
    sub_len = seq // d
    assert window // d == BLK and sub_len % BLK == 0
    qb = _pick(sub_len, (512, 256, 128))
    n_sub = qb // BLK

    qi = np.arange(BLK)[:, None]
    ki = np.arange(2 * BLK)[None, :]
    bucket = _t5_bucket(np.clip(qi + BLK - ki, 0, None) * d)
    buckets = tuple(int(b) for b in np.unique(bucket))

    qkv_v = qkv_g.reshape(batch * d, sub_len, 3 * gw)
    cur = lambda part: (lambda a, n: (a, n, part))
    prev = lambda part: (lambda a, n: (a, jnp.maximum(n * n_sub - 1, 0), part))
    o, lse = pl.pallas_call(
        functools.partial(_attn_kernel, n_heads=n_heads, n_sub=n_sub, buckets=buckets),
        grid=(batch * d, sub_len // qb),
        in_specs=[
            pl.BlockSpec((BLK, 2 * BLK), lambda a, n: (0, 0)),
            pl.BlockSpec(memory_space=pltpu.SMEM),
            pl.BlockSpec((None, qb, gw), cur(0)),
            pl.BlockSpec((None, qb, gw), cur(1)),
            pl.BlockSpec((None, BLK, gw), prev(1)),
            pl.BlockSpec((None, qb, gw), cur(2)),
            pl.BlockSpec((None, BLK, gw), prev(2)),
        ],
        out_specs=[
            pl.BlockSpec((None, qb, gw), lambda a, n: (a, n, 0)),
            pl.BlockSpec((None, qb, LANES), lambda a, n: (a, n, 0)),
        ],
        out_shape=[
            jax.ShapeDtypeStruct((batch * d, sub_len, gw), _BF16),
            jax.ShapeDtypeStruct((batch * d, sub_len, LANES), _F32),
        ],
        scratch_shapes=[pltpu.VMEM((n_heads, BLK, 2 * BLK), _F32)],
        compiler_params=_params(2),
        name=f"attn_d{d}",
    )(jnp.asarray(bucket), table_g, qkv_v, qkv_v, qkv_v, qkv_v, qkv_v)
    return o.reshape(batch, d, sub_len, gw), lse.reshape(batch, d, sub_len, LANES)


def _mix_out_kernel(*refs, n_heads, alpha, dilations):
    ng = len(dilations)
    x_ref, yc_ref = refs[:2]
    o_refs = refs[2:2 + ng]
    l_refs = refs[2 + ng:2 + 2 * ng]
    w_ref, g_ref, b_ref, h_ref = refs[2 + 2 * ng:6 + 2 * ng]
    scratch = list(refs[6 + 2 * ng:])
    tm = x_ref.shape[0]
    head_cols = [slice(h * HEAD_DIM, (h + 1) * HEAD_DIM) for h in range(n_heads)]

    lses, outs = [], []
    for g, d in enumerate(dilations):
        if d == 1:
            lses.append(l_refs[g][0])
            outs.append([o_refs[g][0, :, c].astype(_F32) for c in head_cols])
            continue
        o_nat, l_nat = scratch.pop(0), scratch.pop(0)
        n = tm // d
        for r in range(d):
            l_nat[pl.ds(r, n, stride=d), :] = l_refs[g][r]
            for h, c in enumerate(head_cols):
                o_nat[h, pl.ds(r, n, stride=d), :] = o_refs[g][r, :, c].astype(_F32)
        lses.append(l_nat[...])
        outs.append([o_nat[h] for h in range(n_heads)])

    m = functools.reduce(jnp.maximum, lses)
    es = [jnp.exp(l - m) for l in lses]
    inv = 1.0 / sum(es)
    wts = [e * inv for e in es]
    ya = [sum(wts[g][:, h:h + 1] * outs[g][h] for g in range(ng)) for h in range(n_heads)]
    y_attn = jnp.concatenate(ya, axis=-1).astype(_BF16)
    cw = yc_ref.shape[1]
    mix = _dot(yc_ref[...], w_ref[:cw, :]) + _dot(y_attn, w_ref[cw:, :])
    h_ref[...] = _layernorm(alpha * x_ref[...] + mix, g_ref[...], b_ref[...])


def _mix_out(x2, y_conv, outs, lses, w_out_b, ln_g, ln_b, seq, alpha):
    t, dm = x2.shape
    cw = y_conv.shape[1]
    aw = outs[0].shape[-1]
    n_heads = aw // HEAD_DIM
    dilations = tuple(d for _, d in DIL_PAIRS)
    tm = _pick(seq, (512, 256))
    seq_tiles = seq // tm
    row = lambda w: pl.BlockSpec((tm, w), lambda i: (i, 0))
    grouped = lambda d, w: pl.BlockSpec((None, d, tm // d, w), lambda i: (i // seq_tiles, 0, i % seq_tiles, 0))
    whole = lambda a: pl.BlockSpec(a.shape, lambda i: (0,) * a.ndim)
    scratch = []
    for d in dilations:
        if d > 1:
            scratch += [pltpu.VMEM((n_heads, tm, HEAD_DIM), _F32), pltpu.VMEM((tm, LANES), _F32)]
    return pl.pallas_call(
        functools.partial(_mix_out_kernel, n_heads=n_heads, alpha=alpha, dilations=dilations),
        grid=(t // tm,),
        in_specs=[row(dm), row(cw)] + [grouped(d, aw) for d in dilations] + [grouped(d, LANES) for d in dilations]
                 + [whole(w_out_b), whole(ln_g), whole(ln_b)],
        out_specs=row(dm),
        out_shape=jax.ShapeDtypeStruct((t, dm), _F32),
        scratch_shapes=scratch,
        compiler_params=_params(1),
        name="mix_out",
    )(x2, y_conv, *outs, *lses, w_out_b, ln_g, ln_b)


def _ffn_kernel(h_ref, wa_ref, wg_ref, cwa_ref, cwg_ref, cba_ref, cbg_ref, wd_ref, g_ref, b_ref,
                o_ref, hb_ref, carry_a_ref, carry_g_ref, *, seq_tiles, alpha):
    i = pl.program_id(0)
    j = pl.program_id(1)

    @pl.when(j == 0)
    def _():
        hb_ref[...] = h_ref[...].astype(_BF16)

    hb = hb_ref[...]
    first = i % seq_tiles == 0

    def up_half(w_ref, cw_ref, cb_ref, carry_ref):
        up = _dot(hb, w_ref[...])
        prev = jnp.where(first, 0.0, carry_ref[j])
        carry_ref[j] = up[up.shape[0] - SUBLANES:]
        return _causal_conv3(up, prev, cw_ref[...]) + cb_ref[...]

    a = up_half(wa_ref, cwa_ref, cba_ref, carry_a_ref)
    gate = up_half(wg_ref, cwg_ref, cbg_ref, carry_g_ref)
    act = (gate * (1.0 / (1.0 + jnp.exp(-gate))) * a).astype(_BF16)
    part = _dot(act, wd_ref[...])

    @pl.when(j == 0)
    def _():
        o_ref[...] = part

    @pl.when(j > 0)
    def _():
        o_ref[...] += part

    @pl.when(j == pl.num_programs(1) - 1)
    def _():
        o_ref[...] = _layernorm(alpha * h_ref[...] + o_ref[...], g_ref[...], b_ref[...])


def _conv_ffn(h, w_up_b, conv_w, conv_b, w_down_b, ln_g, ln_b, seq, alpha):
    t, d = h.shape
    d_ff = w_down_b.shape[0]
    tm = _pick(seq, (512, 256, 128))
    tc = _pick(d_ff, (512, 256, 128))
    nj = d_ff // tc
    whole = lambda a: pl.BlockSpec(a.shape, lambda i, j: (0,) * a.ndim)
    return pl.pallas_call(
        functools.partial(_ffn_kernel, seq_tiles=seq // tm, alpha=alpha),
        grid=(t // tm, nj),
        in_specs=[
            pl.BlockSpec((tm, d), lambda i, j: (i, 0)),
            pl.BlockSpec((d, tc), lambda i, j: (0, j)),
            pl.BlockSpec((d, tc), lambda i, j: (0, nj + j)),
            pl.BlockSpec((3, tc), lambda i, j: (0, j)),
            pl.BlockSpec((3, tc), lambda i, j: (0, nj + j)),
            pl.BlockSpec((1, tc), lambda i, j: (0, j)),
            pl.BlockSpec((1, tc), lambda i, j: (0, nj + j)),
            pl.BlockSpec((tc, d), lambda i, j: (j, 0)),
            whole(ln_g), whole(ln_b),
        ],
        out_specs=pl.BlockSpec((tm, d), lambda i, j: (i, 0)),
        out_shape=jax.ShapeDtypeStruct((t, d), _F32),
        scratch_shapes=[pltpu.VMEM((tm, d), _BF16),
                        pltpu.VMEM((nj, SUBLANES, tc), _F32),
                        pltpu.VMEM((nj, SUBLANES, tc), _F32)],
        compiler_params=_params(2),
        name="conv_ffn",
    )(h, w_up_b, w_up_b, conv_w, conv_w, conv_b, conv_b, w_down_b, ln_g, ln_b)


def kernel(x, w_in, conv_mix_w, w_out, ln1_g, ln1_b, w_up, ffn_conv_w, ffn_conv_b, w_down, ln2_g, ln2_b, rel_bias):
    batch, seq, d_model = x.shape
    depth = w_in.shape[0]
    conv_width = conv_mix_w.shape[-1]
    qkv_width = (w_in.shape[-1] - 3 * conv_width) // 3
    gw = qkv_width // N_DIL
    n_heads = gw // HEAD_DIM
    alpha = (2 * depth) ** 0.25
    h = x.reshape(batch * seq, d_model)
    for layer in range(depth):
        w_in_b = w_in[layer].astype(_BF16)
        xs = _x_prep(h, batch, seq)
        y_conv = _conv_mix(xs[1], w_in_b, conv_mix_w[layer], seq, conv_width)
        outs, lses = [], []
        for g, (window, dilation) in enumerate(DIL_PAIRS):
            qkv_g = _qkv_proj(xs[dilation], w_in_b, seq, 3 * conv_width, qkv_width, g, gw)
            o_g, lse_g = _attention_group(qkv_g, rel_bias[:, g * n_heads:(g + 1) * n_heads], batch, seq,
                                          window, dilation)
            outs.append(o_g)
            lses.append(lse_g)
        h = _mix_out(h, y_conv, outs, lses, w_out[layer].astype(_BF16),
                     ln1_g[layer][None], ln1_b[layer][None], seq, alpha)
        h = _conv_ffn(h, w_up[layer].astype(_BF16), ffn_conv_w[layer], ffn_conv_b[layer][None],
                      w_down[layer].astype(_BF16), ln2_g[layer][None], ln2_b[layer][None], seq, alpha)
    return h.reshape(batch, seq, d_model)
```

```python
import functools
import math

import jax
import jax.numpy as jnp
import numpy as np
from jax.experimental import pallas as pl
from jax.experimental.pallas import tpu as pltpu

HEAD_DIM = 128
DIL_PAIRS = ((128, 1), (512, 4), (2048, 16))
N_DIL = len(DIL_PAIRS)
BLK = 128
N_BUCKETS = 32
MAX_DISTANCE = 2048
LN_EPS = 1e-5
LANES = 128
SUBLANES = 8
VMEM_LIMIT_BYTES = 56 * 1024 * 1024
NEG = float(np.finfo(np.float32).min)
EW_ROWS = 64
UP_COLS = 256
DOWN_COLS = 512

_F32 = jnp.float32
_BF16 = jnp.bfloat16


def _params(n_axes):
    return pltpu.CompilerParams(dimension_semantics=("arbitrary",) * n_axes,
                                vmem_limit_bytes=VMEM_LIMIT_BYTES)


def _pick(total, candidates):
    for c in candidates:
        if total % c == 0:
            return c
    raise ValueError(f"no tile in {candidates} divides {total}")


def _dot(a, b):
    return jnp.dot(a, b, preferred_element_type=_F32)


def _shift_rows(p, k, prev):
    rolled = pltpu.roll(p, k, axis=0)
    head_prev = pltpu.roll(prev, k, axis=0)
    row = jax.lax.broadcasted_iota(jnp.int32, head_prev.shape, 0)
    head = jnp.where(row < k, head_prev, rolled[:SUBLANES])
    return jnp.concatenate([head, rolled[SUBLANES:]], axis=0)


def _causal_conv3(p, prev, w):
    return w[0:1] * _shift_rows(p, 2, prev) + w[1:2] * _shift_rows(p, 1, prev) + w[2:3] * p


def _layernorm(z, g, b):
    mu = jnp.mean(z, axis=-1, keepdims=True)
    zc = z - mu
    var = jnp.mean(zc * zc, axis=-1, keepdims=True)
    return zc * jax.lax.rsqrt(var + LN_EPS) * g + b


def _x_prep_kernel(x_ref, nat_ref, *rest, dilations):
    perm_refs, stage_ref = rest[:-1], rest[-1]
    tm, d_model = x_ref.shape
    for c in range(d_model // LANES):
        cols = slice(c * LANES, (c + 1) * LANES)
        v = x_ref[:, cols]
        nat_ref[:, cols] = v.astype(_BF16)
        stage_ref[c] = v
        for d, ref in zip(dilations, perm_refs):
            n = tm // d
            for r in range(d):
                ref[r, :, cols] = stage_ref[c, pl.ds(r, n, stride=d), :].astype(_BF16)


def _x_prep(x2, batch, seq):
    t, dm = x2.shape
    dilations = tuple(d for _, d in DIL_PAIRS if d > 1)
    tm = _pick(seq, (512, 256))
    seq_tiles = seq // tm
    outs = pl.pallas_call(
        functools.partial(_x_prep_kernel, dilations=dilations),
        grid=(t // tm,),
        in_specs=[pl.BlockSpec((tm, dm), lambda i: (i, 0))],
        out_specs=[pl.BlockSpec((tm, dm), lambda i: (i, 0))] + [
            pl.BlockSpec((None, d, tm // d, dm), lambda i: (i // seq_tiles, 0, i % seq_tiles, 0))
            for d in dilations],
        out_shape=[jax.ShapeDtypeStruct((t, dm), _BF16)] + [
            jax.ShapeDtypeStruct((batch, d, seq // d, dm), _BF16) for d in dilations],
        scratch_shapes=[pltpu.VMEM((dm // LANES, tm, LANES), _F32)],
        compiler_params=_params(1),
        name="x_prep",
    )(x2)
    xs = {1: outs[0]}
    for d, o in zip(dilations, outs[1:]):
        xs[d] = o.reshape(t, dm)
    return xs


def _conv_mix_kernel(xb_ref, wu_ref, wb_ref, wc_ref, cw_ref, y_ref, carry_ref, *, seq_tiles):
    i = pl.program_id(0)
    j = pl.program_id(1)
    xb = xb_ref[...]
    p = _dot(xb, wc_ref[...]) * _dot(xb, wu_ref[...])
    prev = jnp.where(i % seq_tiles == 0, 0.0, carry_ref[j])
    carry_ref[j] = p[p.shape[0] - SUBLANES:]
    y = _dot(xb, wb_ref[...]) * _causal_conv3(p, prev, cw_ref[...])
    y_ref[...] = y.astype(y_ref.dtype)


def _conv_mix(xb, w_in_b, conv_w, seq, conv_width):
    t, d = xb.shape
    tm = _pick(seq, (512, 256, 128))
    tc = _pick(conv_width, (512, 256, 128))
    nj = conv_width // tc
    return pl.pallas_call(
        functools.partial(_conv_mix_kernel, seq_tiles=seq // tm),
        grid=(t // tm, nj),
        in_specs=[
            pl.BlockSpec((tm, d), lambda i, j: (i, 0)),
            pl.BlockSpec((d, tc), lambda i, j: (0, j)),
            pl.BlockSpec((d, tc), lambda i, j: (0, nj + j)),
            pl.BlockSpec((d, tc), lambda i, j: (0, 2 * nj + j)),
            pl.BlockSpec((3, tc), lambda i, j: (0, j)),
        ],
        out_specs=pl.BlockSpec((tm, tc), lambda i, j: (i, j)),
        out_shape=jax.ShapeDtypeStruct((t, conv_width), _BF16),
        scratch_shapes=[pltpu.VMEM((nj, SUBLANES, tc), _F32)],
        compiler_params=_params(2),
        name="conv_mix",
    )(xb, w_in_b, w_in_b, w_in_b, conv_w)


def _qkv_kernel(xb_ref, w_ref, o_ref, *, n_q_blocks, scale):
    j = pl.program_id(1)
    acc = _dot(xb_ref[...], w_ref[...])
    acc = acc * jnp.where(j < n_q_blocks, scale, 1.0)
    o_ref[...] = acc.astype(o_ref.dtype)


def _qkv_proj(xb, w_in_b, seq, col0, qkv_width, g, gw):
    t, d = xb.shape
    tm = _pick(seq, (1024, 512, 256, 128))
    tn = _pick(math.gcd(math.gcd(gw, col0), qkv_width), (1024, 512, 256, 128))
    cpp = gw // tn
    col_block = lambda j: (col0 + g * gw) // tn + (j // cpp) * (qkv_width // tn) + j % cpp
    return pl.pallas_call(
        functools.partial(_qkv_kernel, n_q_blocks=cpp, scale=HEAD_DIM ** -0.5),
        grid=(t // tm, 3 * cpp),
        in_specs=[
            pl.BlockSpec((tm, d), lambda i, j: (i, 0)),
            pl.BlockSpec((d, tn), lambda i, j: (0, col_block(j))),
        ],
        out_specs=pl.BlockSpec((tm, tn), lambda i, j: (i, j)),
        out_shape=jax.ShapeDtypeStruct((t, 3 * gw), _BF16),
        compiler_params=_params(2),
        name=f"qkv_proj_g{g}",
    )(xb, w_in_b)


def _t5_bucket(dist):
    max_exact = N_BUCKETS // 2
    n = np.maximum(dist, 1).astype(np.float32)
    large = max_exact + (np.log(n / max_exact) / math.log(MAX_DISTANCE / max_exact)
                         * (N_BUCKETS - max_exact)).astype(np.int32)
    large = np.minimum(large, N_BUCKETS - 1)
    return np.where(dist < max_exact, dist, large).astype(np.int32)


def _attn_kernel(bucket_ref, table_ref, q_ref, kc_ref, kp_ref, vc_ref, vp_ref, o_ref, lse_ref, bias_ref,
                 *, n_heads, n_sub, buckets):
    n = pl.program_id(1)

    @pl.when((pl.program_id(0) == 0) & (n == 0))
    def _():
        bucket = bucket_ref[...]
        for h in range(n_heads):
            acc = jnp.zeros(bucket.shape, _F32)
            for b in buckets:
                acc = jnp.where(bucket == b, table_ref[b, h], acc)
            bias_ref[h] = acc

    row = jax.lax.broadcasted_iota(jnp.int32, (BLK, 2 * BLK), 0)
    col = jax.lax.broadcasted_iota(jnp.int32, (BLK, 2 * BLK), 1)
    steps = row + BLK - col
    valid = (steps >= 0) & (steps <= BLK)
    valid_first = valid & ((col >= BLK) | (n > 0))
    lane = jax.lax.broadcasted_iota(jnp.int32, (BLK, LANES), 1)
    head_cols = [slice(h * HEAD_DIM, (h + 1) * HEAD_DIM) for h in range(n_heads)]
    for s in range(n_sub):
        rows = slice(s * BLK, (s + 1) * BLK)
        krows = slice((s - 1) * BLK, (s + 1) * BLK)
        q3 = jnp.stack([q_ref[rows, c] for c in head_cols])
        if s == 0:
            k3 = jnp.stack([jnp.concatenate([kp_ref[:, c], kc_ref[rows, c]], axis=0) for c in head_cols])
            v3 = jnp.stack([jnp.concatenate([vp_ref[:, c], vc_ref[rows, c]], axis=0) for c in head_cols])
        else:
            k3 = jnp.stack([kc_ref[krows, c] for c in head_cols])
            v3 = jnp.stack([vc_ref[krows, c] for c in head_cols])
        sc = jnp.einsum("hqd,hkd->hqk", q3, k3, preferred_element_type=_F32)
        sc = jnp.where((valid_first if s == 0 else valid)[None], sc + bias_ref[...], NEG)
        m = jnp.max(sc, axis=-1, keepdims=True)
        p = jnp.exp(sc - m)
        den = jnp.sum(p, axis=-1, keepdims=True)
        o = jnp.einsum("hqk,hkd->hqd", p.astype(_BF16), v3, preferred_element_type=_F32)
        o = o * (1.0 / den)
        lse = m + jnp.log(den)
        lse_all = jnp.zeros((BLK, LANES), _F32)
        for h in range(n_heads):
            o_ref[rows, head_cols[h]] = o[h].astype(o_ref.dtype)
            lse_all = jnp.where(lane == h, lse[h], lse_all)
        lse_ref[rows, :] = lse_all


def _attention_group(qkv_g, table_g, batch, seq, window, dilation):
    n_heads = table_g.shape[1]
    gw = n_heads * HEAD_DIM
    d = dilation
    sub_len = seq // d
    assert window // d == BLK and sub_len % BLK == 0
    qb = _pick(sub_len, (512, 256, 128))
    n_sub = qb // BLK

    qi = np.arange(BLK)[:, None]
    ki = np.arange(2 * BLK)[None, :]
    bucket = _t5_bucket(np.clip(qi + BLK - ki, 0, None) * d)
    buckets = tuple(int(b) for b in np.unique(bucket))

    qkv_v = qkv_g.reshape(batch * d, sub_len, 3 * gw)
    cur = lambda part: (lambda a, n: (a, n, part))
    prev = lambda part: (lambda a, n: (a, jnp.maximum(n * n_sub - 1, 0), part))
    o, lse = pl.pallas_call(
        functools.partial(_attn_kernel, n_heads=n_heads, n_sub=n_sub, buckets=buckets),
        grid=(batch * d, sub_len // qb),
        in_specs=[
            pl.BlockSpec((BLK, 2 * BLK), lambda a, n: (0, 0)),
            pl.BlockSpec(memory_space=pltpu.SMEM),
            pl.BlockSpec((None, qb, gw), cur(0)),
            pl.BlockSpec((None, qb, gw), cur(1)),
            pl.BlockSpec((None, BLK, gw), prev(1)),
            pl.BlockSpec((None, qb, gw), cur(2)),
            pl.BlockSpec((None, BLK, gw), prev(2)),
        ],
        out_specs=[
            pl.BlockSpec((None, qb, gw), lambda a, n: (a, n, 0)),
            pl.BlockSpec((None, qb, LANES), lambda a, n: (a, n, 0)),
        ],
        out_shape=[
            jax.ShapeDtypeStruct((batch * d, sub_len, gw), _BF16),
            jax.ShapeDtypeStruct((batch * d, sub_len, LANES), _F32),
        ],
        scratch_shapes=[pltpu.VMEM((n_heads, BLK, 2 * BLK), _F32)],
        compiler_params=_params(2),
        name=f"attn_d{d}",
    )(jnp.asarray(bucket), table_g, qkv_v, qkv_v, qkv_v, qkv_v, qkv_v)
    return o.reshape(batch, d, sub_len, gw), lse.reshape(batch, d, sub_len, LANES)


def _mix_out_kernel(*refs, n_heads, alpha, dilations):
    ng = len(dilations)
    x_ref, yc_ref = refs[:2]
    o_refs = refs[2:2 + ng]
    l_refs = refs[2 + ng:2 + 2 * ng]
    w_ref, g_ref, b_ref, h_ref, hb_ref = refs[2 + 2 * ng:7 + 2 * ng]
    scratch = list(refs[7 + 2 * ng:])
    tm = x_ref.shape[0]
    head_cols = [slice(h * HEAD_DIM, (h + 1) * HEAD_DIM) for h in range(n_heads)]

    lses, outs = [], []
    for g, d in enumerate(dilations):
        if d == 1:
            lses.append(l_refs[g][0])
            outs.append([o_refs[g][0, :, c].astype(_F32) for c in head_cols])
            continue
        o_nat, l_nat = scratch.pop(0), scratch.pop(0)
        n = tm // d
        for r in range(d):
            l_nat[pl.ds(r, n, stride=d), :] = l_refs[g][r]
            for h, c in enumerate(head_cols):
                o_nat[h, pl.ds(r, n, stride=d), :] = o_refs[g][r, :, c].astype(_F32)
        lses.append(l_nat[...])
        outs.append([o_nat[h] for h in range(n_heads)])

    m = functools.reduce(jnp.maximum, lses)
    es = [jnp.exp(l - m) for l in lses]
    inv = 1.0 / sum(es)
    wts = [e * inv for e in es]
    ya = [sum(wts[g][:, h:h + 1] * outs[g][h] for g in range(ng)) for h in range(n_heads)]
    y_attn = jnp.concatenate(ya, axis=-1).astype(_BF16)
    cw = yc_ref.shape[1]
    mix = _dot(yc_ref[...], w_ref[:cw, :]) + _dot(y_attn, w_ref[cw:, :])
    h = _layernorm(alpha * x_ref[...] + mix, g_ref[...], b_ref[...])
    h_ref[...] = h
    hb_ref[...] = h.astype(_BF16)


def _mix_out(x2, y_conv, outs, lses, w_out_b, ln_g, ln_b, seq, alpha):
    t, dm = x2.shape
    cw = y_conv.shape[1]
    aw = outs[0].shape[-1]
    n_heads = aw // HEAD_DIM
    dilations = tuple(d for _, d in DIL_PAIRS)
    tm = _pick(seq, (512, 256))
    seq_tiles = seq // tm
    row = lambda w: pl.BlockSpec((tm, w), lambda i: (i, 0))
    grouped = lambda d, w: pl.BlockSpec((None, d, tm // d, w), lambda i: (i // seq_tiles, 0, i % seq_tiles, 0))
    whole = lambda a: pl.BlockSpec(a.shape, lambda i: (0,) * a.ndim)
    scratch = []
    for d in dilations:
        if d > 1:
            scratch += [pltpu.VMEM((n_heads, tm, HEAD_DIM), _F32), pltpu.VMEM((tm, LANES), _F32)]
    return pl.pallas_call(
        functools.partial(_mix_out_kernel, n_heads=n_heads, alpha=alpha, dilations=dilations),
        grid=(t // tm,),
        in_specs=[row(dm), row(cw)] + [grouped(d, aw) for d in dilations] + [grouped(d, LANES) for d in dilations]
                 + [whole(w_out_b), whole(ln_g), whole(ln_b)],
        out_specs=[row(dm), row(dm)],
        out_shape=[jax.ShapeDtypeStruct((t, dm), _F32), jax.ShapeDtypeStruct((t, dm), _BF16)],
        scratch_shapes=scratch,
        compiler_params=_params(1),
        name="mix_out",
    )(x2, y_conv, *outs, *lses, w_out_b, ln_g, ln_b)


def _ffn_kernel(hb_ref, h_ref, wa_ref, wg_ref, cwa_ref, cwg_ref, cba_ref, cbg_ref, wd_ref, g_ref, b_ref,
                o_ref, ua0_ref, ug0_ref, act0_ref, ua1_ref, ug1_ref, act1_ref, acc_ref, carry_a_ref, carry_g_ref,
                *, nj, seq_tiles, alpha):
    s = pl.program_id(0)
    s1 = jnp.maximum(s - 1, 0)
    s2 = jnp.maximum(s - 2, 0)
    j1 = s1 % nj
    j2 = s2 % nj
    first = (s1 // nj) % seq_tiles == 0

    @pl.when(s == 0)
    def _():
        for ref in (ua1_ref, ug1_ref, act0_ref, acc_ref):
            ref[...] = jnp.zeros_like(ref)

    def step(ua_w, ug_w, ua_r, ug_r, act_w, act_r):
        tm, tc = ua_r.shape
        nd = wd_ref.shape[1]

        def up_piece(w_ref, u_w, c0):
            cols = slice(c0, c0 + UP_COLS)
            u_w[:, cols] = _dot(hb_ref[...], w_ref[:, cols])

        def down_piece(n0):
            cols = slice(n0, min(n0 + DOWN_COLS, nd))
            part = _dot(act_r[...], wd_ref[:, cols])
            acc_ref[:, cols] = jnp.where(j2 == 0, part, acc_ref[:, cols] + part)

        def conv_block(u_ref, cw_ref, cb_ref, carry_ref, r0, cols):
            up = u_ref[r0:r0 + EW_ROWS, cols]
            if r0 == 0:
                prev = jnp.where(first, 0.0, carry_ref[j1, :, cols])
            else:
                prev = u_ref[r0 - SUBLANES:r0, cols]
            return _causal_conv3(up, prev, cw_ref[:, cols]) + cb_ref[:, cols]

        def gate_block(r0, c0):
            cols = slice(c0, c0 + LANES)
            a = conv_block(ua_r, cwa_ref, cba_ref, carry_a_ref, r0, cols)
            gate = conv_block(ug_r, cwg_ref, cbg_ref, carry_g_ref, r0, cols)
            act = gate * (1.0 / (1.0 + jnp.exp(-gate))) * a
            act_w[r0:r0 + EW_ROWS, cols] = act.astype(_BF16)

        mxu = [functools.partial(up_piece, w, u, c0)
               for w, u in ((wa_ref, ua_w), (wg_ref, ug_w)) for c0 in range(0, tc, UP_COLS)]
        mxu += [functools.partial(down_piece, n0) for n0 in range(0, nd, DOWN_COLS)]
        vpu = [functools.partial(gate_block, r0, c0) for c0 in range(0, tc, LANES) for r0 in range(0, tm, EW_ROWS)]
        per = -(-len(vpu) // len(mxu))
        for k, piece in enumerate(mxu):
            piece()
            for block in vpu[k * per:(k + 1) * per]:
                block()
        carry_a_ref[j1] = ua_r[tm - SUBLANES:, :]
        carry_g_ref[j1] = ug_r[tm - SUBLANES:, :]

    @pl.when(s % 2 == 0)
    def _():
        step(ua0_ref, ug0_ref, ua1_ref, ug1_ref, act1_ref, act0_ref)

    @pl.when(s % 2 == 1)
    def _():
        step(ua1_ref, ug1_ref, ua0_ref, ug0_ref, act0_ref, act1_ref)

    @pl.when((s > 1) & (j2 == nj - 1))
    def _():
        o_ref[...] = _layernorm(alpha * h_ref[...] + acc_ref[...], g_ref[...], b_ref[...])


def _conv_ffn(h, hb, w_up_b, conv_w, conv_b, w_down_b, ln_g, ln_b, seq, alpha):
    t, d = h.shape
    d_ff = w_down_b.shape[0]
    tm = _pick(seq, (512, 256, 128))
    tc = _pick(d_ff, (512, 256, 128))
    nj = d_ff // tc
    n_tiles = t // tm
    i0 = lambda s: jnp.minimum(s // nj, n_tiles - 1)
    j1 = lambda s: jnp.maximum(s - 1, 0) % nj
    i2 = lambda s: jnp.maximum(s - 2, 0) // nj
    j2 = lambda s: jnp.maximum(s - 2, 0) % nj
    whole = lambda a: pl.BlockSpec(a.shape, lambda s: (0,) * a.ndim)
    u_buf = pltpu.VMEM((tm, tc), _F32)
    act_buf = pltpu.VMEM((tm, tc), _BF16)
    return pl.pallas_call(
        functools.partial(_ffn_kernel, nj=nj, seq_tiles=seq // tm, alpha=alpha),
        grid=(n_tiles * nj + 2,),
        in_specs=[
            pl.BlockSpec((tm, d), lambda s: (i0(s), 0)),
            pl.BlockSpec((tm, d), lambda s: (i2(s), 0)),
            pl.BlockSpec((d, tc), lambda s: (0, s % nj)),
            pl.BlockSpec((d, tc), lambda s: (0, nj + s % nj)),
            pl.BlockSpec((3, tc), lambda s: (0, j1(s))),
            pl.BlockSpec((3, tc), lambda s: (0, nj + j1(s))),
            pl.BlockSpec((1, tc), lambda s: (0, j1(s))),
            pl.BlockSpec((1, tc), lambda s: (0, nj + j1(s))),
            pl.BlockSpec((tc, d), lambda s: (j2(s), 0)),
            whole(ln_g), whole(ln_b),
        ],
        out_specs=pl.BlockSpec((tm, d), lambda s: (i2(s), 0)),
        out_shape=jax.ShapeDtypeStruct((t, d), _F32),
        scratch_shapes=[u_buf, u_buf, act_buf, u_buf, u_buf, act_buf,
                        pltpu.VMEM((tm, d), _F32),
                        pltpu.VMEM((nj, SUBLANES, tc), _F32),
                        pltpu.VMEM((nj, SUBLANES, tc), _F32)],
        compiler_params=_params(1),
        name="conv_ffn",
    )(hb, h, w_up_b, w_up_b, conv_w, conv_w, conv_b, conv_b, w_down_b, ln_g, ln_b)


def kernel(x, w_in, conv_mix_w, w_out, ln1_g, ln1_b, w_up, ffn_conv_w, ffn_conv_b, w_down, ln2_g, ln2_b, rel_bias):
    batch, seq, d_model = x.shape
    depth = w_in.shape[0]
    conv_width = conv_mix_w.shape[-1]
    qkv_width = (w_in.shape[-1] - 3 * conv_width) // 3
    gw = qkv_width // N_DIL
    n_heads = gw // HEAD_DIM
    alpha = (2 * depth) ** 0.25
    h = x.reshape(batch * seq, d_model)
    for layer in range(depth):
        w_in_b = w_in[layer].astype(_BF16)
        xs = _x_prep(h, batch, seq)
        y_conv = _conv_mix(xs[1], w_in_b, conv_mix_w[layer], seq, conv_width)
        outs, lses = [], []
        for g, (window, dilation) in enumerate(DIL_PAIRS):
            qkv_g = _qkv_proj(xs[dilation], w_in_b, seq, 3 * conv_width, qkv_width, g, gw)
            o_g, lse_g = _attention_group(qkv_g, rel_bias[:, g * n_heads:(g + 1) * n_heads], batch, seq,
                                          window, dilation)
            outs.append(o_g)
            lses.append(lse_g)
        h, hb = _mix_out(h, y_conv, outs, lses, w_out[layer].astype(_BF16),
                         ln1_g[layer][None], ln1_b[layer][None], seq, alpha)
        h = _conv_ffn(h, hb, w_up[layer].astype(_BF16), ffn_conv_w[layer], ffn_conv_b[layer][None],
                      w_down[layer].astype(_BF16), ln2_g[layer][None], ln2_b[layer][None], seq, alpha)
    return h.reshape(batch, seq, d_model)
```

```python
import functools
import math

import jax
import jax.numpy as jnp
import numpy as np
from jax.experimental import pallas as pl
from jax.experimental.pallas import tpu as pltpu

HEAD_DIM = 128
DIL_PAIRS = ((128, 1), (512, 4), (2048, 16))
N_DIL = len(DIL_PAIRS)
BLK = 128
N_BUCKETS = 32
MAX_DISTANCE = 2048
LN_EPS = 1e-5
LANES = 128
SUBLANES = 8
VMEM_LIMIT_BYTES = 56 * 1024 * 1024
NEG = float(np.finfo(np.float32).min)
LOG2E = math.log2(math.e)
LN2 = math.log(2.0)
BLOCKS_PER_PASS = 1
EW_ROWS = 64
UP_COLS = 256
DOWN_COLS = 512
MIX_COLS = 512
LN_ROWS = 32

_F32 = jnp.float32
_BF16 = jnp.bfloat16


def _params(n_axes):
    return pltpu.CompilerParams(dimension_semantics=("arbitrary",) * n_axes,
                                vmem_limit_bytes=VMEM_LIMIT_BYTES)


def _pick(total, candidates):
    for c in candidates:
        if total % c == 0:
            return c
    raise ValueError(f"no tile in {candidates} divides {total}")


def _dot(a, b):
    return jnp.dot(a, b, preferred_element_type=_F32)


def _shift_rows(p, k, prev):
    rolled = pltpu.roll(p, k, axis=0)
    head_prev = pltpu.roll(prev, k, axis=0)
    row = jax.lax.broadcasted_iota(jnp.int32, head_prev.shape, 0)
    head = jnp.where(row < k, head_prev, rolled[:SUBLANES])
    return jnp.concatenate([head, rolled[SUBLANES:]], axis=0)


def _causal_conv3(p, prev, w):
    return w[0:1] * _shift_rows(p, 2, prev) + w[1:2] * _shift_rows(p, 1, prev) + w[2:3] * p


def _layernorm(z, g, b):
    mu = jnp.mean(z, axis=-1, keepdims=True)
    zc = z - mu
    var = jnp.mean(zc * zc, axis=-1, keepdims=True)
    return zc * jax.lax.rsqrt(var + LN_EPS) * g + b


def _x_prep_kernel(x_ref, nat_ref, *rest, dilations):
    perm_refs, stage_ref = rest[:-1], rest[-1]
    tm, d_model = x_ref.shape
    for c in range(d_model // LANES):
        cols = slice(c * LANES, (c + 1) * LANES)
        v = x_ref[:, cols]
        nat_ref[:, cols] = v.astype(_BF16)
        stage_ref[c] = v
        for d, ref in zip(dilations, perm_refs):
            n = tm // d
            for r in range(d):
                ref[r, :, cols] = stage_ref[c, pl.ds(r, n, stride=d), :].astype(_BF16)


def _x_prep(x2, batch, seq):
    t, dm = x2.shape
    dilations = tuple(d for _, d in DIL_PAIRS if d > 1)
    tm = _pick(seq, (512, 256))
    seq_tiles = seq // tm
    outs = pl.pallas_call(
        functools.partial(_x_prep_kernel, dilations=dilations),
        grid=(t // tm,),
        in_specs=[pl.BlockSpec((tm, dm), lambda i: (i, 0))],
        out_specs=[pl.BlockSpec((tm, dm), lambda i: (i, 0))] + [
            pl.BlockSpec((None, d, tm // d, dm), lambda i: (i // seq_tiles, 0, i % seq_tiles, 0))
            for d in dilations],
        out_shape=[jax.ShapeDtypeStruct((t, dm), _BF16)] + [
            jax.ShapeDtypeStruct((batch, d, seq // d, dm), _BF16) for d in dilations],
        scratch_shapes=[pltpu.VMEM((dm // LANES, tm, LANES), _F32)],
        compiler_params=_params(1),
        name="x_prep",
    )(x2)
    xs = {1: outs[0]}
    for d, o in zip(dilations, outs[1:]):
        xs[d] = o.reshape(t, dm)
    return xs


def _conv_mix_kernel(xb_ref, wu_ref, wb_ref, wc_ref, cw_ref, y_ref, carry_ref, *, seq_tiles):
    i = pl.program_id(0)
    j = pl.program_id(1)
    xb = xb_ref[...]
    p = _dot(xb, wc_ref[...]) * _dot(xb, wu_ref[...])
    prev = jnp.where(i % seq_tiles == 0, 0.0, carry_ref[j])
    carry_ref[j] = p[p.shape[0] - SUBLANES:]
    y = _dot(xb, wb_ref[...]) * _causal_conv3(p, prev, cw_ref[...])
    y_ref[...] = y.astype(y_ref.dtype)


def _conv_mix(xb, w_in_b, conv_w, seq, conv_width):
    t, d = xb.shape
    tm = _pick(seq, (512, 256, 128))
    tc = _pick(conv_width, (512, 256, 128))
    nj = conv_width // tc
    return pl.pallas_call(
        functools.partial(_conv_mix_kernel, seq_tiles=seq // tm),
        grid=(t // tm, nj),
        in_specs=[
            pl.BlockSpec((tm, d), lambda i, j: (i, 0)),
            pl.BlockSpec((d, tc), lambda i, j: (0, j)),
            pl.BlockSpec((d, tc), lambda i, j: (0, nj + j)),
            pl.BlockSpec((d, tc), lambda i, j: (0, 2 * nj + j)),
            pl.BlockSpec((3, tc), lambda i, j: (0, j)),
        ],
        out_specs=pl.BlockSpec((tm, tc), lambda i, j: (i, j)),
        out_shape=jax.ShapeDtypeStruct((t, conv_width), _BF16),
        scratch_shapes=[pltpu.VMEM((nj, SUBLANES, tc), _F32)],
        compiler_params=_params(2),
        name="conv_mix",
    )(xb, w_in_b, w_in_b, w_in_b, conv_w)


def _qkv_kernel(xb_ref, w_ref, o_ref, *, n_q_blocks, scale):
    j = pl.program_id(1)
    acc = _dot(xb_ref[...], w_ref[...])
    acc = acc * jnp.where(j < n_q_blocks, scale, 1.0)
    o_ref[...] = acc.astype(o_ref.dtype)


def _qkv_proj(xb, w_in_b, seq, col0, qkv_width, g, gw):
    t, d = xb.shape
    tm = _pick(seq, (2048, 1024, 512, 256, 128))
    tn = _pick(math.gcd(math.gcd(gw, col0), qkv_width), (1024, 512, 256, 128))
    cpp = gw // tn
    col_block = lambda j: (col0 + g * gw) // tn + (j // cpp) * (qkv_width // tn) + j % cpp
    return pl.pallas_call(
        functools.partial(_qkv_kernel, n_q_blocks=cpp, scale=HEAD_DIM ** -0.5 * LOG2E),
        grid=(t // tm, 3 * cpp),
        in_specs=[
            pl.BlockSpec((tm, d), lambda i, j: (i, 0)),
            pl.BlockSpec((d, tn), lambda i, j: (0, col_block(j))),
        ],
        out_specs=pl.BlockSpec((tm, tn), lambda i, j: (i, j)),
        out_shape=jax.ShapeDtypeStruct((t, 3 * gw), _BF16),
        compiler_params=_params(2),
        name=f"qkv_proj_g{g}",
    )(xb, w_in_b)


def _t5_bucket(dist):
    max_exact = N_BUCKETS // 2
    n = np.maximum(dist, 1).astype(np.float32)
    large = max_exact + (np.log(n / max_exact) / math.log(MAX_DISTANCE / max_exact)
                         * (N_BUCKETS - max_exact)).astype(np.int32)
    large = np.minimum(large, N_BUCKETS - 1)
    return np.where(dist < max_exact, dist, large).astype(np.int32)


def _attn_kernel(bucket_ref, table_ref, q_ref, kc_ref, kp_ref, vc_ref, vp_ref, o_ref, lse_ref, bias_ref,
                 *, n_heads, sub_len, buckets):
    n = pl.program_id(0)
    rb = q_ref.shape[0]

    @pl.when(n == 0)
    def _():
        bucket = bucket_ref[...]
        for h in range(n_heads):
            acc = jnp.zeros(bucket.shape, _F32)
            for b in buckets:
                acc = jnp.where(bucket == b, table_ref[b, h] * LOG2E, acc)
            bias_ref[h] = acc

    row = jax.lax.broadcasted_iota(jnp.int32, (BLK, 2 * BLK), 0)
    col = jax.lax.broadcasted_iota(jnp.int32, (BLK, 2 * BLK), 1)
    steps = row + BLK - col
    valid = (steps >= 0) & (steps <= BLK)
    lane = jax.lax.broadcasted_iota(jnp.int32, (BLK, LANES), 1)
    head_cols = [slice(h * HEAD_DIM, (h + 1) * HEAD_DIM) for h in range(n_heads)]
    nt = (((1,), (1,)), ((), ()))

    def rows_of(s):
        return slice(s * BLK, (s + 1) * BLK)

    def keys(cur_ref, prev_ref, s, c):
        if s == 0:
            return jnp.concatenate([prev_ref[:, c], cur_ref[rows_of(0), c]], axis=0)
        return cur_ref[(s - 1) * BLK:(s + 1) * BLK, c]

    def masked_scores(s):
        if (s * BLK) % min(sub_len, rb) != 0:
            mask = valid
        elif sub_len <= rb:
            mask = valid & (col >= BLK)
        else:
            mask = valid & ((col >= BLK) | ((n * rb) % sub_len != 0))
        sc = jnp.stack([jax.lax.dot_general(q_ref[rows_of(s), c], keys(kc_ref, kp_ref, s, c), nt,
                                            preferred_element_type=_F32) for c in head_cols])
        return jnp.where(mask[None], sc + bias_ref[...], NEG)

    n_blocks = rb // BLK
    for s0 in range(0, n_blocks, BLOCKS_PER_PASS):
        ss = range(s0, min(s0 + BLOCKS_PER_PASS, n_blocks))
        sc = jnp.concatenate([masked_scores(s) for s in ss], axis=0)
        m = jnp.max(sc, axis=-1, keepdims=True)
        p = jnp.exp2(sc - m)
        den = jnp.sum(p, axis=-1, keepdims=True)
        p = p.astype(_BF16)
        inv = 1.0 / den
        lse = m * LN2 + jnp.log(den)
        for i, s in enumerate(ss):
            lse_all = jnp.zeros((BLK, LANES), _F32)
            for h, c in enumerate(head_cols):
                u = i * n_heads + h
                o = _dot(p[u], keys(vc_ref, vp_ref, s, c))
                o_ref[rows_of(s), c] = (o * inv[u]).astype(o_ref.dtype)
                lse_all = jnp.where(lane == h, lse[u], lse_all)
            lse_ref[rows_of(s), :] = lse_all


def _attention_group(qkv_g, table_g, batch, seq, window, dilation):
    t = qkv_g.shape[0]
    n_heads = table_g.shape[1]
    gw = n_heads * HEAD_DIM
    d = dilation
    sub_len = seq // d
    assert window // d == BLK and sub_len % BLK == 0
    rb = _pick(t, (1024, 512, 256, 128))
    assert rb % sub_len == 0 or sub_len % rb == 0
    n_sub = rb // BLK

    qi = np.arange(BLK)[:, None]
    ki = np.arange(2 * BLK)[None, :]
    bucket = _t5_bucket(np.clip(qi + BLK - ki, 0, None) * d)
    buckets = tuple(int(b) for b in np.unique(bucket))

    cur = lambda part: (lambda n: (n, part))
    prev = lambda part: (lambda n: (jnp.maximum(n * n_sub - 1, 0), part))
    o, lse = pl.pallas_call(
        functools.partial(_attn_kernel, n_heads=n_heads, sub_len=sub_len, buckets=buckets),
        grid=(t // rb,),
        in_specs=[
            pl.BlockSpec((BLK, 2 * BLK), lambda n: (0, 0)),
            pl.BlockSpec(memory_space=pltpu.SMEM),
            pl.BlockSpec((rb, gw), cur(0)),
            pl.BlockSpec((rb, gw), cur(1)),
            pl.BlockSpec((BLK, gw), prev(1)),
            pl.BlockSpec((rb, gw), cur(2)),
            pl.BlockSpec((BLK, gw), prev(2)),
        ],
        out_specs=[
            pl.BlockSpec((rb, gw), lambda n: (n, 0)),
            pl.BlockSpec((rb, LANES), lambda n: (n, 0)),
        ],
        out_shape=[
            jax.ShapeDtypeStruct((t, gw), _BF16),
            jax.ShapeDtypeStruct((t, LANES), _F32),
        ],
        scratch_shapes=[pltpu.VMEM((n_heads, BLK, 2 * BLK), _F32)],
        compiler_params=_params(1),
        name=f"attn_d{d}",
    )(jnp.asarray(bucket), table_g, qkv_g, qkv_g, qkv_g, qkv_g, qkv_g)
    return o.reshape(batch, d, sub_len, gw), lse.reshape(batch, d, sub_len, LANES)


def _mix_out_kernel(*refs, n_heads, alpha, dilations):
    ng = len(dilations)
    x_ref, yc_ref = refs[:2]
    o_refs = refs[2:2 + ng]
    l_refs = refs[2 + ng:2 + 2 * ng]
    w_ref, g_ref, b_ref, h_ref, hb_ref, ya0_ref, ya1_ref, z0_ref, z1_ref = refs[2 + 2 * ng:11 + 2 * ng]
    regroup = list(refs[11 + 2 * ng:])
    tm, dm = x_ref.shape
    cw = yc_ref.shape[1]
    head_cols = [slice(h * HEAD_DIM, (h + 1) * HEAD_DIM) for h in range(n_heads)]
    bufs = {}
    for d in dilations:
        if d > 1:
            bufs[d] = (regroup.pop(0), regroup.pop(0), regroup.pop(0))

    @pl.when(pl.program_id(0) == 0)
    def _():
        ya1_ref[...] = jnp.zeros_like(ya1_ref)
        z0_ref[...] = jnp.zeros_like(z0_ref)

    def step(ya_w, ya_r, z_w, z_r):
        lses = []
        for g, d in enumerate(dilations):
            if d == 1:
                lses.append(l_refs[g][0])
                continue
            l_nat = bufs[d][2]
            for r in range(d):
                l_nat[pl.ds(r, tm // d, stride=d), :] = l_refs[g][r]
            lses.append(l_nat[...])
        m = functools.reduce(jnp.maximum, lses)
        es = [jnp.exp(l - m) for l in lses]
        inv = 1.0 / sum(es)
        wts = [e * inv for e in es]

        def merge_head(h):
            c = head_cols[h]
            acc = None
            for g, d in enumerate(dilations):
                if d == 1:
                    og = o_refs[g][0, :, c].astype(_F32)
                else:
                    o_nat = bufs[d][h % 2]
                    for r in range(d):
                        o_nat[pl.ds(r, tm // d, stride=d), :] = o_refs[g][r, :, c].astype(_F32)
                    og = o_nat[...]
                term = wts[g][:, h:h + 1] * og
                acc = term if acc is None else acc + term
            ya_w[:, cw + h * HEAD_DIM:cw + (h + 1) * HEAD_DIM] = acc.astype(_BF16)
            conv_cols = slice(h * cw // n_heads, (h + 1) * cw // n_heads)
            ya_w[:, conv_cols] = yc_ref[:, conv_cols]

        def mix_piece(n0):
            cols = slice(n0, min(n0 + MIX_COLS, dm))
            mix = _dot(ya_r[...], w_ref[:, cols])
            z_w[:, cols] = alpha * x_ref[:, cols] + mix

        def norm_rows(r0):
            rows = slice(r0, r0 + LN_ROWS)
            h = _layernorm(z_r[rows, :], g_ref[...], b_ref[...])
            h_ref[rows, :] = h
            hb_ref[rows, :] = h.astype(_BF16)

        mxu = [functools.partial(mix_piece, n0) for n0 in range(0, dm, MIX_COLS)]
        merge = [functools.partial(merge_head, h) for h in range(n_heads)]
        norm = [functools.partial(norm_rows, r0) for r0 in range(0, tm, LN_ROWS)]
        per_merge = -(-len(merge) // len(mxu))
        per_norm = -(-len(norm) // len(mxu))
        for k, piece in enumerate(mxu):
            piece()
            for block in merge[k * per_merge:(k + 1) * per_merge] + norm[k * per_norm:(k + 1) * per_norm]:
                block()

    @pl.when(pl.program_id(0) % 2 == 0)
    def _():
        step(ya0_ref, ya1_ref, z1_ref, z0_ref)

    @pl.when(pl.program_id(0) % 2 == 1)
    def _():
        step(ya1_ref, ya0_ref, z0_ref, z1_ref)


def _mix_out(x2, y_conv, outs, lses, w_out_b, ln_g, ln_b, seq, alpha):
    t, dm = x2.shape
    cw = y_conv.shape[1]
    aw = outs[0].shape[-1]
    n_heads = aw // HEAD_DIM
    dilations = tuple(d for _, d in DIL_PAIRS)
    tm = _pick(seq, (512, 256))
    seq_tiles = seq // tm
    n_tiles = t // tm
    merged = lambda i: jnp.minimum(i, n_tiles - 1)
    mixed = lambda i: jnp.clip(i - 1, 0, n_tiles - 1)
    normed = lambda i: jnp.maximum(i - 2, 0)
    row = lambda w: pl.BlockSpec((tm, w), lambda i: (mixed(i), 0))
    out_row = lambda w: pl.BlockSpec((tm, w), lambda i: (normed(i), 0))
    grouped = lambda d, w: pl.BlockSpec(
        (None, d, tm // d, w), lambda i: (merged(i) // seq_tiles, 0, merged(i) % seq_tiles, 0))
    whole = lambda a: pl.BlockSpec(a.shape, lambda i: (0,) * a.ndim)
    scratch = [pltpu.VMEM((tm, cw + aw), _BF16), pltpu.VMEM((tm, cw + aw), _BF16),
               pltpu.VMEM((tm, dm), _F32), pltpu.VMEM((tm, dm), _F32)]
    for d in dilations:
        if d > 1:
            scratch += [pltpu.VMEM((tm, HEAD_DIM), _F32), pltpu.VMEM((tm, HEAD_DIM), _F32),
                        pltpu.VMEM((tm, LANES), _F32)]
    return pl.pallas_call(
        functools.partial(_mix_out_kernel, n_heads=n_heads, alpha=alpha, dilations=dilations),
        grid=(n_tiles + 2,),
        in_specs=[row(dm), pl.BlockSpec((tm, cw), lambda i: (merged(i), 0))]
                 + [grouped(d, aw) for d in dilations] + [grouped(d, LANES) for d in dilations]
                 + [whole(w_out_b), whole(ln_g), whole(ln_b)],
        out_specs=[out_row(dm), out_row(dm)],
        out_shape=[jax.ShapeDtypeStruct((t, dm), _F32), jax.ShapeDtypeStruct((t, dm), _BF16)],
        scratch_shapes=scratch,
        compiler_params=_params(1),
        name="mix_out",
    )(x2, y_conv, *outs, *lses, w_out_b, ln_g, ln_b)


def _ffn_kernel(hb_ref, h_ref, wa_ref, wg_ref, cwa_ref, cwg_ref, cba_ref, cbg_ref, wd_ref, g_ref, b_ref,
                o_ref, ua0_ref, ug0_ref, act0_ref, ua1_ref, ug1_ref, act1_ref, acc_ref, carry_a_ref, carry_g_ref,
                *, nj, seq_tiles, alpha):
    s = pl.program_id(0)
    s1 = jnp.maximum(s - 1, 0)
    s2 = jnp.maximum(s - 2, 0)
    j1 = s1 % nj
    j2 = s2 % nj
    first = (s1 // nj) % seq_tiles == 0

    @pl.when(s == 0)
    def _():
        for ref in (ua1_ref, ug1_ref, act0_ref, acc_ref):
            ref[...] = jnp.zeros_like(ref)

    def step(ua_w, ug_w, ua_r, ug_r, act_w, act_r):
        tm, tc = ua_r.shape
        nd = wd_ref.shape[1]

        def up_piece(w_ref, u_w, c0):
            cols = slice(c0, c0 + UP_COLS)
            u_w[:, cols] = _dot(hb_ref[...], w_ref[:, cols])

        def down_piece(n0):
            cols = slice(n0, min(n0 + DOWN_COLS, nd))
            part = _dot(act_r[...], wd_ref[:, cols])
            acc_ref[:, cols] = jnp.where(j2 == 0, part, acc_ref[:, cols] + part)

        def conv_block(u_ref, cw_ref, cb_ref, carry_ref, r0, cols):
            up = u_ref[r0:r0 + EW_ROWS, cols]
            if r0 == 0:
                prev = jnp.where(first, 0.0, carry_ref[j1, :, cols])
            else:
                prev = u_ref[r0 - SUBLANES:r0, cols]
            return _causal_conv3(up, prev, cw_ref[:, cols]) + cb_ref[:, cols]

        def gate_block(r0, c0):
            cols = slice(c0, c0 + LANES)
            a = conv_block(ua_r, cwa_ref, cba_ref, carry_a_ref, r0, cols)
            gate = conv_block(ug_r, cwg_ref, cbg_ref, carry_g_ref, r0, cols)
            act = gate * (1.0 / (1.0 + jnp.exp(-gate))) * a
            act_w[r0:r0 + EW_ROWS, cols] = act.astype(_BF16)

        mxu = [functools.partial(up_piece, w, u, c0)
               for w, u in ((wa_ref, ua_w), (wg_ref, ug_w)) for c0 in range(0, tc, UP_COLS)]
        mxu += [functools.partial(down_piece, n0) for n0 in range(0, nd, DOWN_COLS)]
        vpu = [functools.partial(gate_block, r0, c0) for c0 in range(0, tc, LANES) for r0 in range(0, tm, EW_ROWS)]
        per = -(-len(vpu) // len(mxu))
        for k, piece in enumerate(mxu):
            piece()
            for block in vpu[k * per:(k + 1) * per]:
                block()
        carry_a_ref[j1] = ua_r[tm - SUBLANES:, :]
        carry_g_ref[j1] = ug_r[tm - SUBLANES:, :]

    @pl.when(s % 2 == 0)
    def _():
        step(ua0_ref, ug0_ref, ua1_ref, ug1_ref, act1_ref, act0_ref)

    @pl.when(s % 2 == 1)
    def _():
        step(ua1_ref, ug1_ref, ua0_ref, ug0_ref, act0_ref, act1_ref)

    @pl.when((s > 1) & (j2 == nj - 1))
    def _():
        o_ref[...] = _layernorm(alpha * h_ref[...] + acc_ref[...], g_ref[...], b_ref[...])


def _conv_ffn(h, hb, w_up_b, conv_w, conv_b, w_down_b, ln_g, ln_b, seq, alpha):
    t, d = h.shape
    d_ff = w_down_b.shape[0]
    tm = _pick(seq, (512, 256, 128))
    tc = _pick(d_ff, (512, 256, 128))
    nj = d_ff // tc
    n_tiles = t // tm
    i0 = lambda s: jnp.minimum(s // nj, n_tiles - 1)
    j1 = lambda s: jnp.maximum(s - 1, 0) % nj
    i2 = lambda s: jnp.maximum(s - 2, 0) // nj
    j2 = lambda s: jnp.maximum(s - 2, 0) % nj
    whole = lambda a: pl.BlockSpec(a.shape, lambda s: (0,) * a.ndim)
    u_buf = pltpu.VMEM((tm, tc), _F32)
    act_buf = pltpu.VMEM((tm, tc), _BF16)
    return pl.pallas_call(
        functools.partial(_ffn_kernel, nj=nj, seq_tiles=seq // tm, alpha=alpha),
        grid=(n_tiles * nj + 2,),
        in_specs=[
            pl.BlockSpec((tm, d), lambda s: (i0(s), 0)),
            pl.BlockSpec((tm, d), lambda s: (i2(s), 0)),
            pl.BlockSpec((d, tc), lambda s: (0, s % nj)),
            pl.BlockSpec((d, tc), lambda s: (0, nj + s % nj)),
            pl.BlockSpec((3, tc), lambda s: (0, j1(s))),
            pl.BlockSpec((3, tc), lambda s: (0, nj + j1(s))),
            pl.BlockSpec((1, tc), lambda s: (0, j1(s))),
            pl.BlockSpec((1, tc), lambda s: (0, nj + j1(s))),
            pl.BlockSpec((tc, d), lambda s: (j2(s), 0)),
            whole(ln_g), whole(ln_b),
        ],
        out_specs=pl.BlockSpec((tm, d), lambda s: (i2(s), 0)),
        out_shape=jax.ShapeDtypeStruct((t, d), _F32),
        scratch_shapes=[u_buf, u_buf, act_buf, u_buf, u_buf, act_buf,
                        pltpu.VMEM((tm, d), _F32),
                        pltpu.VMEM((nj, SUBLANES, tc), _F32),
                        pltpu.VMEM((nj, SUBLANES, tc), _F32)],
        compiler_params=_params(1),
        name="conv_ffn",
    )(hb, h, w_up_b, w_up_b, conv_w, conv_w, conv_b, conv_b, w_down_b, ln_g, ln_b)


def kernel(x, w_in, conv_mix_w, w_out, ln1_g, ln1_b, w_up, ffn_conv_w, ffn_conv_b, w_down, ln2_g, ln2_b, rel_bias):
    batch, seq, d_model = x.shape
    depth = w_in.shape[0]
    conv_width = conv_mix_w.shape[-1]
    qkv_width = (w_in.shape[-1] - 3 * conv_width) // 3
    gw = qkv_width // N_DIL
    n_heads = gw // HEAD_DIM
    alpha = (2 * depth) ** 0.25
    h = x.reshape(batch * seq, d_model)
    for layer in range(depth):
        w_in_b = w_in[layer].astype(_BF16)
        xs = _x_prep(h, batch, seq)
        y_conv = _conv_mix(xs[1], w_in_b, conv_mix_w[layer], seq, conv_width)
        outs, lses = [], []
        for g, (window, dilation) in enumerate(DIL_PAIRS):
            qkv_g = _qkv_proj(xs[dilation], w_in_b, seq, 3 * conv_width, qkv_width, g, gw)
            o_g, lse_g = _attention_group(qkv_g, rel_bias[:, g * n_heads:(g + 1) * n_heads], batch, seq,
                                          window, dilation)
            outs.append(o_g)
            lses.append(lse_g)
        h, hb = _mix_out(h, y_conv, outs, lses, w_out[layer].astype(_BF16),
                         ln1_g[layer][None], ln1_b[layer][None], seq, alpha)
        h = _conv_ffn(h, hb, w_up[layer].astype(_BF16), ffn_conv_w[layer], ffn_conv_b[layer][None],
                      w_down[layer].astype(_BF16), ln2_g[layer][None], ln2_b[layer][None], seq, alpha)
    return h.reshape(batch, seq, d_model)
```

```python
import functools
import math

import jax
import jax.numpy as jnp
import numpy as np
from jax.experimental import pallas as pl
from jax.experimental.pallas import tpu as pltpu

HEAD_DIM = 128
DIL_PAIRS = ((128, 1), (512, 4), (2048, 16))
N_DIL = len(DIL_PAIRS)
BLK = 128
N_BUCKETS = 32
MAX_DISTANCE = 2048
LN_EPS = 1e-5
LANES = 128
SUBLANES = 8
VMEM_LIMIT_BYTES = 56 * 1024 * 1024
NEG = float(np.finfo(np.float32).min)
LOG2E = math.log2(math.e)
LN2 = math.log(2.0)
BLOCKS_PER_PASS = 1
EW_ROWS = 64
UP_COLS = 256
DOWN_COLS = 512
MIX_COLS = 512
MERGE_HEADS = 2
LN_ROWS = 32

_F32 = jnp.float32
_BF16 = jnp.bfloat16


def _params(n_axes):
    return pltpu.CompilerParams(dimension_semantics=("arbitrary",) * n_axes,
                                vmem_limit_bytes=VMEM_LIMIT_BYTES)


def _pick(total, candidates):
    for c in candidates:
        if total % c == 0:
            return c
    raise ValueError(f"no tile in {candidates} divides {total}")


def _dot(a, b):
    return jnp.dot(a, b, preferred_element_type=_F32)


def _shift_rows(p, k, prev):
    rolled = pltpu.roll(p, k, axis=0)
    head_prev = pltpu.roll(prev, k, axis=0)
    row = jax.lax.broadcasted_iota(jnp.int32, head_prev.shape, 0)
    head = jnp.where(row < k, head_prev, rolled[:SUBLANES])
    return jnp.concatenate([head, rolled[SUBLANES:]], axis=0)


def _causal_conv3(p, prev, w):
    return w[0:1] * _shift_rows(p, 2, prev) + w[1:2] * _shift_rows(p, 1, prev) + w[2:3] * p


def _layernorm(z, g, b):
    mu = jnp.mean(z, axis=-1, keepdims=True)
    zc = z - mu
    var = jnp.mean(zc * zc, axis=-1, keepdims=True)
    return zc * jax.lax.rsqrt(var + LN_EPS) * g + b


def _x_prep_kernel(x_ref, nat_ref, *rest, dilations):
    perm_refs, stage_ref = rest[:-1], rest[-1]
    tm, d_model = x_ref.shape
    for c in range(d_model // LANES):
        cols = slice(c * LANES, (c + 1) * LANES)
        v = x_ref[:, cols]
        nat_ref[:, cols] = v.astype(_BF16)
        stage_ref[c] = v
        for d, ref in zip(dilations, perm_refs):
            n = tm // d
            for r in range(d):
                ref[r, :, cols] = stage_ref[c, pl.ds(r, n, stride=d), :].astype(_BF16)


def _x_prep(x2, batch, seq):
    t, dm = x2.shape
    dilations = tuple(d for _, d in DIL_PAIRS if d > 1)
    tm = _pick(seq, (512, 256))
    seq_tiles = seq // tm
    outs = pl.pallas_call(
        functools.partial(_x_prep_kernel, dilations=dilations),
        grid=(t // tm,),
        in_specs=[pl.BlockSpec((tm, dm), lambda i: (i, 0))],
        out_specs=[pl.BlockSpec((tm, dm), lambda i: (i, 0))] + [
            pl.BlockSpec((None, d, tm // d, dm), lambda i: (i // seq_tiles, 0, i % seq_tiles, 0))
            for d in dilations],
        out_shape=[jax.ShapeDtypeStruct((t, dm), _BF16)] + [
            jax.ShapeDtypeStruct((batch, d, seq // d, dm), _BF16) for d in dilations],
        scratch_shapes=[pltpu.VMEM((dm // LANES, tm, LANES), _F32)],
        compiler_params=_params(1),
        name="x_prep",
    )(x2)
    xs = {1: outs[0]}
    for d, o in zip(dilations, outs[1:]):
        xs[d] = o.reshape(t, dm)
    return xs


def _conv_mix_kernel(xb_ref, wu_ref, wb_ref, wc_ref, cw_ref, y_ref, carry_ref, *, seq_tiles):
    i = pl.program_id(0)
    j = pl.program_id(1)
    xb = xb_ref[...]
    p = _dot(xb, wc_ref[...]) * _dot(xb, wu_ref[...])
    prev = jnp.where(i % seq_tiles == 0, 0.0, carry_ref[j])
    carry_ref[j] = p[p.shape[0] - SUBLANES:]
    y = _dot(xb, wb_ref[...]) * _causal_conv3(p, prev, cw_ref[...])
    y_ref[...] = y.astype(y_ref.dtype)


def _conv_mix(xb, w_in_b, conv_w, seq, conv_width):
    t, d = xb.shape
    tm = _pick(seq, (512, 256, 128))
    tc = _pick(conv_width, (512, 256, 128))
    nj = conv_width // tc
    return pl.pallas_call(
        functools.partial(_conv_mix_kernel, seq_tiles=seq // tm),
        grid=(t // tm, nj),
        in_specs=[
            pl.BlockSpec((tm, d), lambda i, j: (i, 0)),
            pl.BlockSpec((d, tc), lambda i, j: (0, j)),
            pl.BlockSpec((d, tc), lambda i, j: (0, nj + j)),
            pl.BlockSpec((d, tc), lambda i, j: (0, 2 * nj + j)),
            pl.BlockSpec((3, tc), lambda i, j: (0, j)),
        ],
        out_specs=pl.BlockSpec((tm, tc), lambda i, j: (i, j)),
        out_shape=jax.ShapeDtypeStruct((t, conv_width), _BF16),
        scratch_shapes=[pltpu.VMEM((nj, SUBLANES, tc), _F32)],
        compiler_params=_params(2),
        name="conv_mix",
    )(xb, w_in_b, w_in_b, w_in_b, conv_w)


def _qkv_kernel(xb_ref, w_ref, o_ref, *, n_q_blocks, scale):
    j = pl.program_id(1)
    acc = _dot(xb_ref[...], w_ref[...])
    acc = acc * jnp.where(j < n_q_blocks, scale, 1.0)
    o_ref[...] = acc.astype(o_ref.dtype)


def _qkv_proj(xb, w_in_b, seq, col0, qkv_width, g, gw):
    t, d = xb.shape
    tm = _pick(seq, (2048, 1024, 512, 256, 128))
    tn = _pick(math.gcd(math.gcd(gw, col0), qkv_width), (1024, 512, 256, 128))
    cpp = gw // tn
    col_block = lambda j: (col0 + g * gw) // tn + (j // cpp) * (qkv_width // tn) + j % cpp
    return pl.pallas_call(
        functools.partial(_qkv_kernel, n_q_blocks=cpp, scale=HEAD_DIM ** -0.5 * LOG2E),
        grid=(t // tm, 3 * cpp),
        in_specs=[
            pl.BlockSpec((tm, d), lambda i, j: (i, 0)),
            pl.BlockSpec((d, tn), lambda i, j: (0, col_block(j))),
        ],
        out_specs=pl.BlockSpec((tm, tn), lambda i, j: (i, j)),
        out_shape=jax.ShapeDtypeStruct((t, 3 * gw), _BF16),
        compiler_params=_params(2),
        name=f"qkv_proj_g{g}",
    )(xb, w_in_b)


def _t5_bucket(dist):
    max_exact = N_BUCKETS // 2
    n = np.maximum(dist, 1).astype(np.float32)
    large = max_exact + (np.log(n / max_exact) / math.log(MAX_DISTANCE / max_exact)
                         * (N_BUCKETS - max_exact)).astype(np.int32)
    large = np.minimum(large, N_BUCKETS - 1)
    return np.where(dist < max_exact, dist, large).astype(np.int32)


def _attn_kernel(bucket_ref, table_ref, q_ref, kc_ref, kp_ref, vc_ref, vp_ref, o_ref, lse_ref, bias_ref,
                 *, n_heads, sub_len, buckets):
    n = pl.program_id(0)
    rb = q_ref.shape[0]

    @pl.when(n == 0)
    def _():
        bucket = bucket_ref[...]
        for h in range(n_heads):
            acc = jnp.zeros(bucket.shape, _F32)
            for b in buckets:
                acc = jnp.where(bucket == b, table_ref[b, h] * LOG2E, acc)
            bias_ref[h] = acc

    row = jax.lax.broadcasted_iota(jnp.int32, (BLK, 2 * BLK), 0)
    col = jax.lax.broadcasted_iota(jnp.int32, (BLK, 2 * BLK), 1)
    steps = row + BLK - col
    valid = (steps >= 0) & (steps <= BLK)
    lane = jax.lax.broadcasted_iota(jnp.int32, (BLK, LANES), 1)
    head_cols = [slice(h * HEAD_DIM, (h + 1) * HEAD_DIM) for h in range(n_heads)]
    nt = (((1,), (1,)), ((), ()))

    def rows_of(s):
        return slice(s * BLK, (s + 1) * BLK)

    def keys(cur_ref, prev_ref, s, c):
        if s == 0:
            return jnp.concatenate([prev_ref[:, c], cur_ref[rows_of(0), c]], axis=0)
        return cur_ref[(s - 1) * BLK:(s + 1) * BLK, c]

    def masked_scores(s):
        if (s * BLK) % min(sub_len, rb) != 0:
            mask = valid
        elif sub_len <= rb:
            mask = valid & (col >= BLK)
        else:
            mask = valid & ((col >= BLK) | ((n * rb) % sub_len != 0))
        sc = jnp.stack([jax.lax.dot_general(q_ref[rows_of(s), c], keys(kc_ref, kp_ref, s, c), nt,
                                            preferred_element_type=_F32) for c in head_cols])
        return jnp.where(mask[None], sc + bias_ref[...], NEG)

    n_blocks = rb // BLK
    for s0 in range(0, n_blocks, BLOCKS_PER_PASS):
        ss = range(s0, min(s0 + BLOCKS_PER_PASS, n_blocks))
        sc = jnp.concatenate([masked_scores(s) for s in ss], axis=0)
        m = jnp.max(sc, axis=-1, keepdims=True)
        p = jnp.exp2(sc - m)
        den = jnp.sum(p, axis=-1, keepdims=True)
        p = p.astype(_BF16)
        inv = 1.0 / den
        lse = m * LN2 + jnp.log(den)
        for i, s in enumerate(ss):
            lse_all = jnp.zeros((BLK, LANES), _F32)
            for h, c in enumerate(head_cols):
                u = i * n_heads + h
                o = _dot(p[u], keys(vc_ref, vp_ref, s, c))
                o_ref[rows_of(s), c] = (o * inv[u]).astype(o_ref.dtype)
                lse_all = jnp.where(lane == h, lse[u], lse_all)
            lse_ref[rows_of(s), :] = lse_all


def _attention_group(qkv_g, table_g, batch, seq, window, dilation):
    t = qkv_g.shape[0]
    n_heads = table_g.shape[1]
    gw = n_heads * HEAD_DIM
    d = dilation
    sub_len = seq // d
    assert window // d == BLK and sub_len % BLK == 0
    rb = _pick(t, (1024, 512, 256, 128))
    assert rb % sub_len == 0 or sub_len % rb == 0
    n_sub = rb // BLK

    qi = np.arange(BLK)[:, None]
    ki = np.arange(2 * BLK)[None, :]
    bucket = _t5_bucket(np.clip(qi + BLK - ki, 0, None) * d)
    buckets = tuple(int(b) for b in np.unique(bucket))

    cur = lambda part: (lambda n: (n, part))
    prev = lambda part: (lambda n: (jnp.maximum(n * n_sub - 1, 0), part))
    o, lse = pl.pallas_call(
        functools.partial(_attn_kernel, n_heads=n_heads, sub_len=sub_len, buckets=buckets),
        grid=(t // rb,),
        in_specs=[
            pl.BlockSpec((BLK, 2 * BLK), lambda n: (0, 0)),
            pl.BlockSpec(memory_space=pltpu.SMEM),
            pl.BlockSpec((rb, gw), cur(0)),
            pl.BlockSpec((rb, gw), cur(1)),
            pl.BlockSpec((BLK, gw), prev(1)),
            pl.BlockSpec((rb, gw), cur(2)),
            pl.BlockSpec((BLK, gw), prev(2)),
        ],
        out_specs=[
            pl.BlockSpec((rb, gw), lambda n: (n, 0)),
            pl.BlockSpec((rb, LANES), lambda n: (n, 0)),
        ],
        out_shape=[
            jax.ShapeDtypeStruct((t, gw), _BF16),
            jax.ShapeDtypeStruct((t, LANES), _F32),
        ],
        scratch_shapes=[pltpu.VMEM((n_heads, BLK, 2 * BLK), _F32)],
        compiler_params=_params(1),
        name=f"attn_d{d}",
    )(jnp.asarray(bucket), table_g, qkv_g, qkv_g, qkv_g, qkv_g, qkv_g)
    return o.reshape(batch, d, sub_len, gw), lse.reshape(batch, d, sub_len, LANES)


def _mix_out_kernel(*refs, n_heads, alpha, dilations):
    ng = len(dilations)
    nperm = sum(d > 1 for d in dilations)
    x_ref, yc_ref = refs[:2]
    o_refs = refs[2:2 + ng]
    l_refs = refs[2 + ng:2 + 2 * ng]
    perm_refs = dict(zip([d for d in dilations if d > 1], refs[2 + 2 * ng:2 + 2 * ng + nperm]))
    rest = refs[2 + 2 * ng + nperm:]
    w_ref, g_ref, b_ref, h_ref, hb_ref, ya0_ref, ya1_ref, z0_ref, z1_ref = rest[:9]
    l_nats = dict(zip(perm_refs, rest[9:]))
    tm, dm = x_ref.shape
    cw = yc_ref.shape[1]

    @pl.when(pl.program_id(0) == 0)
    def _():
        ya1_ref[...] = jnp.zeros_like(ya1_ref)
        z0_ref[...] = jnp.zeros_like(z0_ref)

    def step(ya_w, ya_r, z_w, z_r):
        lses = []
        for g, d in enumerate(dilations):
            if d == 1:
                lses.append(l_refs[g][0])
                continue
            for r in range(d):
                l_nats[d][pl.ds(r, tm // d, stride=d), :] = l_refs[g][r]
            lses.append(l_nats[d][...])
        m = functools.reduce(jnp.maximum, lses)
        es = [jnp.exp(l - m) for l in lses]
        inv = 1.0 / sum(es)
        wts = [e * inv for e in es]

        def merge_heads(h0):
            width = MERGE_HEADS * HEAD_DIM
            cols = slice(h0 * HEAD_DIM, h0 * HEAD_DIM + width)
            acc = [None] * MERGE_HEADS
            for g, d in enumerate(dilations):
                if d == 1:
                    og = o_refs[g][0, :, cols].astype(_F32)
                else:
                    og = _dot(perm_refs[d][...], o_refs[g][:, :, cols].reshape(tm, width))
                for k in range(MERGE_HEADS):
                    term = wts[g][:, h0 + k:h0 + k + 1] * og[:, k * HEAD_DIM:(k + 1) * HEAD_DIM]
                    acc[k] = term if acc[k] is None else acc[k] + term
            for k in range(MERGE_HEADS):
                c0 = cw + (h0 + k) * HEAD_DIM
                ya_w[:, c0:c0 + HEAD_DIM] = acc[k].astype(_BF16)
            conv_cols = slice(h0 * cw // n_heads, (h0 + MERGE_HEADS) * cw // n_heads)
            ya_w[:, conv_cols] = yc_ref[:, conv_cols]

        def mix_piece(n0):
            cols = slice(n0, min(n0 + MIX_COLS, dm))
            mix = _dot(ya_r[...], w_ref[:, cols])
            z_w[:, cols] = alpha * x_ref[:, cols] + mix

        def norm_rows(r0):
            rows = slice(r0, r0 + LN_ROWS)
            h = _layernorm(z_r[rows, :], g_ref[...], b_ref[...])
            h_ref[rows, :] = h
            hb_ref[rows, :] = h.astype(_BF16)

        mxu = [functools.partial(mix_piece, n0) for n0 in range(0, dm, MIX_COLS)]
        merge = [functools.partial(merge_heads, h0) for h0 in range(0, n_heads, MERGE_HEADS)]
        norm = [functools.partial(norm_rows, r0) for r0 in range(0, tm, LN_ROWS)]
        per_merge = -(-len(merge) // len(mxu))
        per_norm = -(-len(norm) // len(mxu))
        for k, piece in enumerate(mxu):
            piece()
            for block in merge[k * per_merge:(k + 1) * per_merge] + norm[k * per_norm:(k + 1) * per_norm]:
                block()

    @pl.when(pl.program_id(0) % 2 == 0)
    def _():
        step(ya0_ref, ya1_ref, z1_ref, z0_ref)

    @pl.when(pl.program_id(0) % 2 == 1)
    def _():
        step(ya1_ref, ya0_ref, z0_ref, z1_ref)


def _mix_out(x2, y_conv, outs, lses, w_out_b, ln_g, ln_b, seq, alpha):
    t, dm = x2.shape
    cw = y_conv.shape[1]
    aw = outs[0].shape[-1]
    n_heads = aw // HEAD_DIM
    dilations = tuple(d for _, d in DIL_PAIRS)
    tm = _pick(seq, (512, 256))
    seq_tiles = seq // tm
    n_tiles = t // tm
    merged = lambda i: jnp.minimum(i, n_tiles - 1)
    mixed = lambda i: jnp.clip(i - 1, 0, n_tiles - 1)
    normed = lambda i: jnp.maximum(i - 2, 0)
    row = lambda w: pl.BlockSpec((tm, w), lambda i: (mixed(i), 0))
    out_row = lambda w: pl.BlockSpec((tm, w), lambda i: (normed(i), 0))
    grouped = lambda d, w: pl.BlockSpec(
        (None, d, tm // d, w), lambda i: (merged(i) // seq_tiles, 0, merged(i) % seq_tiles, 0))
    whole = lambda a: pl.BlockSpec(a.shape, lambda i: (0,) * a.ndim)
    scratch = [pltpu.VMEM((tm, cw + aw), _BF16), pltpu.VMEM((tm, cw + aw), _BF16),
               pltpu.VMEM((tm, dm), _F32), pltpu.VMEM((tm, dm), _F32)]
    perms = []
    for d in dilations:
        if d > 1:
            scratch.append(pltpu.VMEM((tm, LANES), _F32))
            tok = np.arange(tm)
            perm = np.zeros((tm, tm), np.float32)
            perm[tok, (tok % d) * (tm // d) + tok // d] = 1.0
            perms.append(jnp.asarray(perm, _BF16))
    return pl.pallas_call(
        functools.partial(_mix_out_kernel, n_heads=n_heads, alpha=alpha, dilations=dilations),
        grid=(n_tiles + 2,),
        in_specs=[row(dm), pl.BlockSpec((tm, cw), lambda i: (merged(i), 0))]
                 + [grouped(d, aw) for d in dilations] + [grouped(d, LANES) for d in dilations]
                 + [whole(p) for p in perms] + [whole(w_out_b), whole(ln_g), whole(ln_b)],
        out_specs=[out_row(dm), out_row(dm)],
        out_shape=[jax.ShapeDtypeStruct((t, dm), _F32), jax.ShapeDtypeStruct((t, dm), _BF16)],
        scratch_shapes=scratch,
        compiler_params=_params(1),
        name="mix_out",
    )(x2, y_conv, *outs, *lses, *perms, w_out_b, ln_g, ln_b)


def _ffn_kernel(hb_ref, h_ref, wa_ref, wg_ref, cw_ref, cb_ref, wd_ref, g_ref, b_ref,
                o_ref, ua0_ref, ug0_ref, act0_ref, ua1_ref, ug1_ref, act1_ref, acc_ref, carry_a_ref, carry_g_ref,
                *, nj, seq_tiles, alpha):
    s = pl.program_id(0)
    s1 = jnp.maximum(s - 1, 0)
    s2 = jnp.maximum(s - 2, 0)
    j1 = s1 % nj
    j2 = s2 % nj
    first = (s1 // nj) % seq_tiles == 0

    @pl.when(s == 0)
    def _():
        for ref in (ua1_ref, ug1_ref, act0_ref, acc_ref):
            ref[...] = jnp.zeros_like(ref)

    def step(ua_w, ug_w, ua_r, ug_r, act_w, act_r):
        tm, tc = ua_r.shape
        nd = wd_ref.shape[1]

        def up_piece(w_ref, u_w, c0):
            cols = slice(c0, c0 + UP_COLS)
            u_w[:, cols] = _dot(hb_ref[...], w_ref[:, cols])

        def down_piece(n0):
            cols = slice(n0, min(n0 + DOWN_COLS, nd))
            part = _dot(act_r[...], wd_ref[:, cols])
            acc_ref[:, cols] = jnp.where(j2 == 0, part, acc_ref[:, cols] + part)

        def conv_block(u_ref, chunk, carry_ref, r0, cols):
            up = u_ref[r0:r0 + EW_ROWS, cols]
            if r0 == 0:
                prev = jnp.where(first, 0.0, carry_ref[j1, :, cols])
            else:
                prev = u_ref[r0 - SUBLANES:r0, cols]
            return _causal_conv3(up, prev, cw_ref[chunk, :, cols]) + cb_ref[chunk, :, cols]

        def gate_block(r0, c0):
            cols = slice(c0, c0 + LANES)
            a = conv_block(ua_r, j1, carry_a_ref, r0, cols)
            gate = conv_block(ug_r, nj + j1, carry_g_ref, r0, cols)
            act = gate * (1.0 / (1.0 + jnp.exp(-gate))) * a
            act_w[r0:r0 + EW_ROWS, cols] = act.astype(_BF16)

        mxu = [functools.partial(up_piece, w, u, c0)
               for w, u in ((wa_ref, ua_w), (wg_ref, ug_w)) for c0 in range(0, tc, UP_COLS)]
        mxu += [functools.partial(down_piece, n0) for n0 in range(0, nd, DOWN_COLS)]
        vpu = [functools.partial(gate_block, r0, c0) for c0 in range(0, tc, LANES) for r0 in range(0, tm, EW_ROWS)]
        per = -(-len(vpu) // len(mxu))
        for k, piece in enumerate(mxu):
            piece()
            for block in vpu[k * per:(k + 1) * per]:
                block()
        carry_a_ref[j1] = ua_r[tm - SUBLANES:, :]
        carry_g_ref[j1] = ug_r[tm - SUBLANES:, :]

    @pl.when(s % 2 == 0)
    def _():
        step(ua0_ref, ug0_ref, ua1_ref, ug1_ref, act1_ref, act0_ref)

    @pl.when(s % 2 == 1)
    def _():
        step(ua1_ref, ug1_ref, ua0_ref, ug0_ref, act0_ref, act1_ref)

    @pl.when((s > 1) & (j2 == nj - 1))
    def _():
        o_ref[...] = _layernorm(alpha * h_ref[...] + acc_ref[...], g_ref[...], b_ref[...])


def _conv_ffn(h, hb, w_up_b, conv_w, conv_b, w_down_b, ln_g, ln_b, seq, alpha):
    t, d = h.shape
    d_ff = w_down_b.shape[0]
    tm = _pick(seq, (512, 256, 128))
    tc = _pick(d_ff, (512, 256, 128))
    nj = d_ff // tc
    n_tiles = t // tm
    i0 = lambda s: jnp.minimum(s // nj, n_tiles - 1)
    j1 = lambda s: jnp.maximum(s - 1, 0) % nj
    i2 = lambda s: jnp.maximum(s - 2, 0) // nj
    j2 = lambda s: jnp.maximum(s - 2, 0) % nj
    whole = lambda a: pl.BlockSpec(a.shape, lambda s: (0,) * a.ndim)
    conv_w_chunks = conv_w.reshape(3, 2 * nj, tc).transpose(1, 0, 2)
    conv_b_chunks = conv_b.reshape(1, 2 * nj, tc).transpose(1, 0, 2)
    u_buf = pltpu.VMEM((tm, tc), _F32)
    act_buf = pltpu.VMEM((tm, tc), _BF16)
    return pl.pallas_call(
        functools.partial(_ffn_kernel, nj=nj, seq_tiles=seq // tm, alpha=alpha),
        grid=(n_tiles * nj + 2,),
        in_specs=[
            pl.BlockSpec((tm, d), lambda s: (i0(s), 0)),
            pl.BlockSpec((tm, d), lambda s: (i2(s), 0)),
            pl.BlockSpec((d, tc), lambda s: (0, s % nj)),
            pl.BlockSpec((d, tc), lambda s: (0, nj + s % nj)),
            whole(conv_w_chunks), whole(conv_b_chunks),
            pl.BlockSpec((tc, d), lambda s: (j2(s), 0)),
            whole(ln_g), whole(ln_b),
        ],
        out_specs=pl.BlockSpec((tm, d), lambda s: (i2(s), 0)),
        out_shape=jax.ShapeDtypeStruct((t, d), _F32),
        scratch_shapes=[u_buf, u_buf, act_buf, u_buf, u_buf, act_buf,
                        pltpu.VMEM((tm, d), _F32),
                        pltpu.VMEM((nj, SUBLANES, tc), _F32),
                        pltpu.VMEM((nj, SUBLANES, tc), _F32)],
        compiler_params=_params(1),
        name="conv_ffn",
    )(hb, h, w_up_b, w_up_b, conv_w_chunks, conv_b_chunks, w_down_b, ln_g, ln_b)


def kernel(x, w_in, conv_mix_w, w_out, ln1_g, ln1_b, w_up, ffn_conv_w, ffn_conv_b, w_down, ln2_g, ln2_b, rel_bias):
    batch, seq, d_model = x.shape
    depth = w_in.shape[0]
    conv_width = conv_mix_w.shape[-1]
    qkv_width = (w_in.shape[-1] - 3 * conv_width) // 3
    gw = qkv_width // N_DIL
    n_heads = gw // HEAD_DIM
    alpha = (2 * depth) ** 0.25
    h = x.reshape(batch * seq, d_model)
    for layer in range(depth):
        w_in_b = w_in[layer].astype(_BF16)
        xs = _x_prep(h, batch, seq)
        y_conv = _conv_mix(xs[1], w_in_b, conv_mix_w[layer], seq, conv_width)
        outs, lses = [], []
        for g, (window, dilation) in enumerate(DIL_PAIRS):
            qkv_g = _qkv_proj(xs[dilation], w_in_b, seq, 3 * conv_width, qkv_width, g, gw)
            o_g, lse_g = _attention_group(qkv_g, rel_bias[:, g * n_heads:(g + 1) * n_heads], batch, seq,
                                          window, dilation)
            outs.append(o_g)
            lses.append(lse_g)
        h, hb = _mix_out(h, y_conv, outs, lses, w_out[layer].astype(_BF16),
                         ln1_g[layer][None], ln1_b[layer][None], seq, alpha)
        h = _conv_ffn(h, hb, w_up[layer].astype(_BF16), ffn_conv_w[layer], ffn_conv_b[layer][None],
                      w_down[layer].astype(_BF16), ln2_g[layer][None], ln2_b[layer][None], seq, alpha)
    return h.reshape(batch, seq, d_model)
```

```python
import functools
import math

import jax
import jax.numpy as jnp
import numpy as np
from jax.experimental import pallas as pl
from jax.experimental.pallas import tpu as pltpu

HEAD_DIM = 128
DIL_PAIRS = ((128, 1), (512, 4), (2048, 16))
N_DIL = len(DIL_PAIRS)
BLK = 128
N_BUCKETS = 32
MAX_DISTANCE = 2048
LN_EPS = 1e-5
LANES = 128
SUBLANES = 8
VMEM_LIMIT_BYTES = 56 * 1024 * 1024
NEG = float(np.finfo(np.float32).min)
LOG2E = math.log2(math.e)
LN2 = math.log(2.0)
BLOCKS_PER_PASS = 1
EW_ROWS = 64
UP_COLS = 256
DOWN_COLS = 512
MIX_COLS = 512
MERGE_HEADS = 2
LN_ROWS = 32

_F32 = jnp.float32
_BF16 = jnp.bfloat16


def _params(n_axes):
    return pltpu.CompilerParams(dimension_semantics=("arbitrary",) * n_axes,
                                vmem_limit_bytes=VMEM_LIMIT_BYTES)


def _pick(total, candidates):
    for c in candidates:
        if total % c == 0:
            return c
    raise ValueError(f"no tile in {candidates} divides {total}")


def _dot(a, b):
    return jnp.dot(a, b, preferred_element_type=_F32)


def _shift_rows(p, k, prev):
    rolled = pltpu.roll(p, k, axis=0)
    head_prev = pltpu.roll(prev, k, axis=0)
    row = jax.lax.broadcasted_iota(jnp.int32, head_prev.shape, 0)
    head = jnp.where(row < k, head_prev, rolled[:SUBLANES])
    return jnp.concatenate([head, rolled[SUBLANES:]], axis=0)


def _causal_conv3(p, prev, w):
    return w[0:1] * _shift_rows(p, 2, prev) + w[1:2] * _shift_rows(p, 1, prev) + w[2:3] * p


def _layernorm(z, g, b):
    mu = jnp.mean(z, axis=-1, keepdims=True)
    zc = z - mu
    var = jnp.mean(zc * zc, axis=-1, keepdims=True)
    return zc * jax.lax.rsqrt(var + LN_EPS) * g + b


def _x_prep_kernel(x_ref, nat_ref, *rest, dilations):
    perm_refs, stage_ref = rest[:-1], rest[-1]
    tm, d_model = x_ref.shape
    for c in range(d_model // LANES):
        cols = slice(c * LANES, (c + 1) * LANES)
        v = x_ref[:, cols]
        nat_ref[:, cols] = v.astype(_BF16)
        stage_ref[c] = v
        for d, ref in zip(dilations, perm_refs):
            n = tm // d
            for r in range(d):
                ref[r, :, cols] = stage_ref[c, pl.ds(r, n, stride=d), :].astype(_BF16)


def _x_prep(x2, batch, seq):
    t, dm = x2.shape
    dilations = tuple(d for _, d in DIL_PAIRS if d > 1)
    tm = _pick(seq, (512, 256))
    seq_tiles = seq // tm
    outs = pl.pallas_call(
        functools.partial(_x_prep_kernel, dilations=dilations),
        grid=(t // tm,),
        in_specs=[pl.BlockSpec((tm, dm), lambda i: (i, 0))],
        out_specs=[pl.BlockSpec((tm, dm), lambda i: (i, 0))] + [
            pl.BlockSpec((None, d, tm // d, dm), lambda i: (i // seq_tiles, 0, i % seq_tiles, 0))
            for d in dilations],
        out_shape=[jax.ShapeDtypeStruct((t, dm), _BF16)] + [
            jax.ShapeDtypeStruct((batch, d, seq // d, dm), _BF16) for d in dilations],
        scratch_shapes=[pltpu.VMEM((dm // LANES, tm, LANES), _F32)],
        compiler_params=_params(1),
        name="x_prep",
    )(x2)
    xs = {1: outs[0]}
    for d, o in zip(dilations, outs[1:]):
        xs[d] = o.reshape(t, dm)
    return xs


def _conv_mix_kernel(xb_ref, wu_ref, wb_ref, wc_ref, cw_ref, y_ref, carry_ref, *, seq_tiles):
    i = pl.program_id(0)
    j = pl.program_id(1)
    xb = xb_ref[...]
    p = _dot(xb, wc_ref[...]) * _dot(xb, wu_ref[...])
    prev = jnp.where(i % seq_tiles == 0, 0.0, carry_ref[j])
    carry_ref[j] = p[p.shape[0] - SUBLANES:]
    y = _dot(xb, wb_ref[...]) * _causal_conv3(p, prev, cw_ref[...])
    y_ref[...] = y.astype(y_ref.dtype)


def _conv_mix(xb, w_in_b, conv_w, seq, conv_width):
    t, d = xb.shape
    tm = _pick(seq, (512, 256, 128))
    tc = _pick(conv_width, (512, 256, 128))
    nj = conv_width // tc
    return pl.pallas_call(
        functools.partial(_conv_mix_kernel, seq_tiles=seq // tm),
        grid=(t // tm, nj),
        in_specs=[
            pl.BlockSpec((tm, d), lambda i, j: (i, 0)),
            pl.BlockSpec((d, tc), lambda i, j: (0, j)),
            pl.BlockSpec((d, tc), lambda i, j: (0, nj + j)),
            pl.BlockSpec((d, tc), lambda i, j: (0, 2 * nj + j)),
            pl.BlockSpec((3, tc), lambda i, j: (0, j)),
        ],
        out_specs=pl.BlockSpec((tm, tc), lambda i, j: (i, j)),
        out_shape=jax.ShapeDtypeStruct((t, conv_width), _BF16),
        scratch_shapes=[pltpu.VMEM((nj, SUBLANES, tc), _F32)],
        compiler_params=_params(2),
        name="conv_mix",
    )(xb, w_in_b, w_in_b, w_in_b, conv_w)


def _qkv_kernel(xb_ref, w_ref, o_ref, *, n_q_blocks, scale):
    j = pl.program_id(1)
    acc = _dot(xb_ref[...], w_ref[...])
    acc = acc * jnp.where(j < n_q_blocks, scale, 1.0)
    o_ref[...] = acc.astype(o_ref.dtype)


def _qkv_proj(xb, w_in_b, seq, col0, qkv_width, g, gw):
    t, d = xb.shape
    tm = _pick(seq, (2048, 1024, 512, 256, 128))
    tn = _pick(math.gcd(math.gcd(gw, col0), qkv_width), (1024, 512, 256, 128))
    cpp = gw // tn
    col_block = lambda j: (col0 + g * gw) // tn + (j // cpp) * (qkv_width // tn) + j % cpp
    return pl.pallas_call(
        functools.partial(_qkv_kernel, n_q_blocks=cpp, scale=HEAD_DIM ** -0.5 * LOG2E),
        grid=(t // tm, 3 * cpp),
        in_specs=[
            pl.BlockSpec((tm, d), lambda i, j: (i, 0)),
            pl.BlockSpec((d, tn), lambda i, j: (0, col_block(j))),
        ],
        out_specs=pl.BlockSpec((tm, tn), lambda i, j: (i, j)),
        out_shape=jax.ShapeDtypeStruct((t, 3 * gw), _BF16),
        compiler_params=_params(2),
        name=f"qkv_proj_g{g}",
    )(xb, w_in_b)


def _t5_bucket(dist):
    max_exact = N_BUCKETS // 2
    n = np.maximum(dist, 1).astype(np.float32)
    large = max_exact + (np.log(n / max_exact) / math.log(MAX_DISTANCE / max_exact)
                         * (N_BUCKETS - max_exact)).astype(np.int32)
    large = np.minimum(large, N_BUCKETS - 1)
    return np.where(dist < max_exact, dist, large).astype(np.int32)


def _attn_kernel(bucket_ref, table_ref, q_ref, kc_ref, kp_ref, vc_ref, vp_ref, o_ref, lse_ref, bias_ref,
                 *, n_heads, sub_len, buckets):
    n = pl.program_id(0)
    rb = q_ref.shape[0]

    @pl.when(n == 0)
    def _():
        bucket = bucket_ref[...]
        for h in range(n_heads):
            acc = jnp.zeros(bucket.shape, _F32)
            for b in buckets:
                acc = jnp.where(bucket == b, table_ref[b, h] * LOG2E, acc)
            bias_ref[h] = acc

    row = jax.lax.broadcasted_iota(jnp.int32, (BLK, 2 * BLK), 0)
    col = jax.lax.broadcasted_iota(jnp.int32, (BLK, 2 * BLK), 1)
    steps = row + BLK - col
    valid = (steps >= 0) & (steps <= BLK)
    lane = jax.lax.broadcasted_iota(jnp.int32, (BLK, LANES), 1)
    head_cols = [slice(h * HEAD_DIM, (h + 1) * HEAD_DIM) for h in range(n_heads)]
    nt = (((1,), (1,)), ((), ()))

    def rows_of(s):
        return slice(s * BLK, (s + 1) * BLK)

    def keys(cur_ref, prev_ref, s, c):
        if s == 0:
            return jnp.concatenate([prev_ref[:, c], cur_ref[rows_of(0), c]], axis=0)
        return cur_ref[(s - 1) * BLK:(s + 1) * BLK, c]

    def masked_scores(s):
        if (s * BLK) % min(sub_len, rb) != 0:
            mask = valid
        elif sub_len <= rb:
            mask = valid & (col >= BLK)
        else:
            mask = valid & ((col >= BLK) | ((n * rb) % sub_len != 0))
        sc = jnp.stack([jax.lax.dot_general(q_ref[rows_of(s), c], keys(kc_ref, kp_ref, s, c), nt,
                                            preferred_element_type=_F32) for c in head_cols])
        return jnp.where(mask[None], sc + bias_ref[...], NEG)

    n_blocks = rb // BLK
    for s0 in range(0, n_blocks, BLOCKS_PER_PASS):
        ss = range(s0, min(s0 + BLOCKS_PER_PASS, n_blocks))
        sc = jnp.concatenate([masked_scores(s) for s in ss], axis=0)
        m = jnp.max(sc, axis=-1, keepdims=True)
        p = jnp.exp2(sc - m)
        den = jnp.sum(p, axis=-1, keepdims=True)
        p = p.astype(_BF16)
        inv = 1.0 / den
        lse = m * LN2 + jnp.log(den)
        for i, s in enumerate(ss):
            lse_all = jnp.zeros((BLK, LANES), _F32)
            for h, c in enumerate(head_cols):
                u = i * n_heads + h
                o = _dot(p[u], keys(vc_ref, vp_ref, s, c))
                o_ref[rows_of(s), c] = (o * inv[u]).astype(o_ref.dtype)
                lse_all = jnp.where(lane == h, lse[u], lse_all)
            lse_ref[rows_of(s), :] = lse_all


def _attention_group(qkv_g, table_g, batch, seq, window, dilation):
    t = qkv_g.shape[0]
    n_heads = table_g.shape[1]
    gw = n_heads * HEAD_DIM
    d = dilation
    sub_len = seq // d
    assert window // d == BLK and sub_len % BLK == 0
    rb = _pick(t, (1024, 512, 256, 128))
    assert rb % sub_len == 0 or sub_len % rb == 0
    n_sub = rb // BLK

    qi = np.arange(BLK)[:, None]
    ki = np.arange(2 * BLK)[None, :]
    bucket = _t5_bucket(np.clip(qi + BLK - ki, 0, None) * d)
    buckets = tuple(int(b) for b in np.unique(bucket))

    cur = lambda part: (lambda n: (n, part))
    prev = lambda part: (lambda n: (jnp.maximum(n * n_sub - 1, 0), part))
    o, lse = pl.pallas_call(
        functools.partial(_attn_kernel, n_heads=n_heads, sub_len=sub_len, buckets=buckets),
        grid=(t // rb,),
        in_specs=[
            pl.BlockSpec((BLK, 2 * BLK), lambda n: (0, 0)),
            pl.BlockSpec(memory_space=pltpu.SMEM),
            pl.BlockSpec((rb, gw), cur(0)),
            pl.BlockSpec((rb, gw), cur(1)),
            pl.BlockSpec((BLK, gw), prev(1)),
            pl.BlockSpec((rb, gw), cur(2)),
            pl.BlockSpec((BLK, gw), prev(2)),
        ],
        out_specs=[
            pl.BlockSpec((rb, gw), lambda n: (n, 0)),
            pl.BlockSpec((rb, LANES), lambda n: (n, 0)),
        ],
        out_shape=[
            jax.ShapeDtypeStruct((t, gw), _BF16),
            jax.ShapeDtypeStruct((t, LANES), _F32),
        ],
        scratch_shapes=[pltpu.VMEM((n_heads, BLK, 2 * BLK), _F32)],
        compiler_params=_params(1),
        name=f"attn_d{d}",
    )(jnp.asarray(bucket), table_g, qkv_g, qkv_g, qkv_g, qkv_g, qkv_g)
    return o.reshape(batch, d, sub_len, gw), lse.reshape(batch, d, sub_len, LANES)


def _mix_out_kernel(*refs, n_heads, alpha, dilations):
    ng = len(dilations)
    nperm = sum(d > 1 for d in dilations)
    x_ref, yc_ref = refs[:2]
    o_refs = refs[2:2 + ng]
    l_refs = refs[2 + ng:2 + 2 * ng]
    perm_refs = dict(zip([d for d in dilations if d > 1], refs[2 + 2 * ng:2 + 2 * ng + nperm]))
    rest = refs[2 + 2 * ng + nperm:]
    w_ref, g_ref, b_ref, h_ref, hb_ref, ya0_ref, ya1_ref, z0_ref, z1_ref = rest[:9]
    l_nats = dict(zip(perm_refs, rest[9:]))
    tm, dm = x_ref.shape
    cw = yc_ref.shape[1]

    @pl.when(pl.program_id(0) == 0)
    def _():
        ya1_ref[...] = jnp.zeros_like(ya1_ref)
        z0_ref[...] = jnp.zeros_like(z0_ref)

    def step(ya_w, ya_r, z_w, z_r):
        lses = []
        for g, d in enumerate(dilations):
            if d == 1:
                lses.append(l_refs[g][0])
                continue
            for r in range(d):
                l_nats[d][pl.ds(r, tm // d, stride=d), :] = l_refs[g][r]
            lses.append(l_nats[d][...])
        m = functools.reduce(jnp.maximum, lses)
        es = [jnp.exp(l - m) for l in lses]
        inv = 1.0 / sum(es)
        wts = [e * inv for e in es]

        def merge_heads(h0):
            width = MERGE_HEADS * HEAD_DIM
            cols = slice(h0 * HEAD_DIM, h0 * HEAD_DIM + width)
            acc = [None] * MERGE_HEADS
            for g, d in enumerate(dilations):
                if d == 1:
                    og = o_refs[g][0, :, cols].astype(_F32)
                else:
                    og = _dot(perm_refs[d][...], o_refs[g][:, :, cols].reshape(tm, width))
                for k in range(MERGE_HEADS):
                    term = wts[g][:, h0 + k:h0 + k + 1] * og[:, k * HEAD_DIM:(k + 1) * HEAD_DIM]
                    acc[k] = term if acc[k] is None else acc[k] + term
            for k in range(MERGE_HEADS):
                c0 = cw + (h0 + k) * HEAD_DIM
                ya_w[:, c0:c0 + HEAD_DIM] = acc[k].astype(_BF16)
            conv_cols = slice(h0 * cw // n_heads, (h0 + MERGE_HEADS) * cw // n_heads)
            ya_w[:, conv_cols] = yc_ref[:, conv_cols]

        def mix_piece(n0):
            cols = slice(n0, min(n0 + MIX_COLS, dm))
            mix = _dot(ya_r[...], w_ref[:, cols])
            z_w[:, cols] = alpha * x_ref[:, cols] + mix

        def norm_rows(r0):
            rows = slice(r0, r0 + LN_ROWS)
            h = _layernorm(z_r[rows, :], g_ref[...], b_ref[...])
            h_ref[rows, :] = h
            hb_ref[rows, :] = h.astype(_BF16)

        mxu = [functools.partial(mix_piece, n0) for n0 in range(0, dm, MIX_COLS)]
        merge = [functools.partial(merge_heads, h0) for h0 in range(0, n_heads, MERGE_HEADS)]
        norm = [functools.partial(norm_rows, r0) for r0 in range(0, tm, LN_ROWS)]
        per_merge = -(-len(merge) // len(mxu))
        per_norm = -(-len(norm) // len(mxu))
        for k, piece in enumerate(mxu):
            piece()
            for block in merge[k * per_merge:(k + 1) * per_merge] + norm[k * per_norm:(k + 1) * per_norm]:
                block()

    @pl.when(pl.program_id(0) % 2 == 0)
    def _():
        step(ya0_ref, ya1_ref, z1_ref, z0_ref)

    @pl.when(pl.program_id(0) % 2 == 1)
    def _():
        step(ya1_ref, ya0_ref, z0_ref, z1_ref)


def _mix_out(x2, y_conv, outs, lses, w_out_b, ln_g, ln_b, seq, alpha):
    t, dm = x2.shape
    cw = y_conv.shape[1]
    aw = outs[0].shape[-1]
    n_heads = aw // HEAD_DIM
    dilations = tuple(d for _, d in DIL_PAIRS)
    tm = _pick(seq, (512, 256))
    seq_tiles = seq // tm
    n_tiles = t // tm
    merged = lambda i: jnp.minimum(i, n_tiles - 1)
    mixed = lambda i: jnp.clip(i - 1, 0, n_tiles - 1)
    normed = lambda i: jnp.maximum(i - 2, 0)
    row = lambda w: pl.BlockSpec((tm, w), lambda i: (mixed(i), 0))
    out_row = lambda w: pl.BlockSpec((tm, w), lambda i: (normed(i), 0))
    grouped = lambda d, w: pl.BlockSpec(
        (None, d, tm // d, w), lambda i: (merged(i) // seq_tiles, 0, merged(i) % seq_tiles, 0))
    whole = lambda a: pl.BlockSpec(a.shape, lambda i: (0,) * a.ndim)
    scratch = [pltpu.VMEM((tm, cw + aw), _BF16), pltpu.VMEM((tm, cw + aw), _BF16),
               pltpu.VMEM((tm, dm), _F32), pltpu.VMEM((tm, dm), _F32)]
    perms = []
    for d in dilations:
        if d > 1:
            scratch.append(pltpu.VMEM((tm, LANES), _F32))
            tok = np.arange(tm)
            perm = np.zeros((tm, tm), np.float32)
            perm[tok, (tok % d) * (tm // d) + tok // d] = 1.0
            perms.append(jnp.asarray(perm, _BF16))
    return pl.pallas_call(
        functools.partial(_mix_out_kernel, n_heads=n_heads, alpha=alpha, dilations=dilations),
        grid=(n_tiles + 2,),
        in_specs=[row(dm), pl.BlockSpec((tm, cw), lambda i: (merged(i), 0))]
                 + [grouped(d, aw) for d in dilations] + [grouped(d, LANES) for d in dilations]
                 + [whole(p) for p in perms] + [whole(w_out_b), whole(ln_g), whole(ln_b)],
        out_specs=[out_row(dm), out_row(dm)],
        out_shape=[jax.ShapeDtypeStruct((t, dm), _F32), jax.ShapeDtypeStruct((t, dm), _BF16)],
        scratch_shapes=scratch,
        compiler_params=_params(1),
        name="mix_out",
    )(x2, y_conv, *outs, *lses, *perms, w_out_b, ln_g, ln_b)


def _ffn_kernel(hb_ref, h_ref, wa_ref, wg_ref, cw_ref, cb_ref, wd_ref, g_ref, b_ref,
                o_hbm, ua0_ref, ug0_ref, act0_ref, ua1_ref, ug1_ref, act1_ref, acc_ref, carry_a_ref, carry_g_ref,
                in_sem, out_sem, *, nj, n_tiles, seq_tiles, alpha):
    s = pl.program_id(0)
    s1 = jnp.maximum(s - 1, 0)
    s2 = jnp.maximum(s - 2, 0)
    j1 = s1 % nj
    j2 = s2 % nj
    tile2 = s2 // nj
    slot2 = tile2 % 2
    first = (s1 // nj) % seq_tiles == 0
    tm = acc_ref.shape[1]

    def load_rows(tile):
        return pltpu.make_async_copy(h_ref.at[pl.ds(tile * tm, tm)], acc_ref.at[tile % 2], in_sem.at[tile % 2])

    def store_rows(tile):
        return pltpu.make_async_copy(acc_ref.at[tile % 2], o_hbm.at[pl.ds(tile * tm, tm)], out_sem.at[tile % 2])

    @pl.when(s == 0)
    def _():
        for ref in (ua1_ref, ug1_ref, act0_ref):
            ref[...] = jnp.zeros_like(ref)
        acc_ref[1] = jnp.zeros(acc_ref.shape[1:], _F32)
        load_rows(0).start()

    @pl.when((s >= 2) & (j2 == 0))
    def _():
        load_rows(tile2).wait()

    @pl.when((s >= 2) & (j2 == 1))
    def _():
        @pl.when(tile2 >= 1)
        def _():
            store_rows(tile2 - 1).wait()

        @pl.when(tile2 + 1 < n_tiles)
        def _():
            load_rows(tile2 + 1).start()

    def step(ua_w, ug_w, ua_r, ug_r, act_w, act_r):
        tc = ua_r.shape[1]
        nd = wd_ref.shape[1]
        acc = acc_ref.at[jnp.where(s < 2, 1, slot2)]
        keep = jnp.where(j2 == 0, alpha, 1.0)

        def up_piece(w_ref, u_w, c0):
            cols = slice(c0, c0 + UP_COLS)
            u_w[:, cols] = _dot(hb_ref[...], w_ref[:, cols])

        def down_piece(n0):
            cols = slice(n0, min(n0 + DOWN_COLS, nd))
            part = _dot(act_r[...], wd_ref[:, cols])
            acc[:, cols] = keep * acc[:, cols] + part

        def conv_block(u_ref, chunk, carry_ref, r0, cols):
            up = u_ref[r0:r0 + EW_ROWS, cols]
            if r0 == 0:
                prev = jnp.where(first, 0.0, carry_ref[j1, :, cols])
            else:
                prev = u_ref[r0 - SUBLANES:r0, cols]
            return _causal_conv3(up, prev, cw_ref[chunk, :, cols]) + cb_ref[chunk, :, cols]

        def gate_block(r0, c0):
            cols = slice(c0, c0 + LANES)
            a = conv_block(ua_r, j1, carry_a_ref, r0, cols)
            gate = conv_block(ug_r, nj + j1, carry_g_ref, r0, cols)
            act = gate * (1.0 / (1.0 + jnp.exp(-gate))) * a
            act_w[r0:r0 + EW_ROWS, cols] = act.astype(_BF16)

        mxu = [functools.partial(down_piece, n0) for n0 in range(0, nd, DOWN_COLS)]
        mxu += [functools.partial(up_piece, w, u, c0)
                for w, u in ((wa_ref, ua_w), (wg_ref, ug_w)) for c0 in range(0, tc, UP_COLS)]
        vpu = [functools.partial(gate_block, r0, c0) for c0 in range(0, tc, LANES) for r0 in range(0, tm, EW_ROWS)]
        per = -(-len(vpu) // len(mxu))
        for k, piece in enumerate(mxu):
            piece()
            for block in vpu[k * per:(k + 1) * per]:
                block()
        carry_a_ref[j1] = ua_r[tm - SUBLANES:, :]
        carry_g_ref[j1] = ug_r[tm - SUBLANES:, :]

    @pl.when(s % 2 == 0)
    def _():
        step(ua0_ref, ug0_ref, ua1_ref, ug1_ref, act1_ref, act0_ref)

    @pl.when(s % 2 == 1)
    def _():
        step(ua1_ref, ug1_ref, ua0_ref, ug0_ref, act0_ref, act1_ref)

    @pl.when((s > 1) & (j2 == nj - 1))
    def _():
        acc = acc_ref.at[slot2]
        for r0 in range(0, tm, LN_ROWS):
            rows = slice(r0, r0 + LN_ROWS)
            acc[rows, :] = _layernorm(acc[rows, :], g_ref[...], b_ref[...])
        store_rows(tile2).start()

        @pl.when(tile2 == n_tiles - 1)
        def _():
            store_rows(tile2).wait()


def _conv_ffn(h, hb, w_up_b, conv_w, conv_b, w_down_b, ln_g, ln_b, seq, alpha):
    t, d = h.shape
    d_ff = w_down_b.shape[0]
    tm = _pick(seq, (1024, 512, 256, 128))
    tc = _pick(d_ff, (512, 256, 128))
    nj = d_ff // tc
    n_tiles = t // tm
    i0 = lambda s: jnp.minimum(s // nj, n_tiles - 1)
    j2 = lambda s: jnp.maximum(s - 2, 0) % nj
    whole = lambda a: pl.BlockSpec(a.shape, lambda s: (0,) * a.ndim)
    conv_w_chunks = conv_w.reshape(3, 2 * nj, tc).transpose(1, 0, 2)
    conv_b_chunks = conv_b.reshape(1, 2 * nj, tc).transpose(1, 0, 2)
    u_buf = pltpu.VMEM((tm, tc), _F32)
    act_buf = pltpu.VMEM((tm, tc), _BF16)
    return pl.pallas_call(
        functools.partial(_ffn_kernel, nj=nj, n_tiles=n_tiles, seq_tiles=seq // tm, alpha=alpha),
        grid=(n_tiles * nj + 2,),
        in_specs=[
            pl.BlockSpec((tm, d), lambda s: (i0(s), 0)),
            pl.BlockSpec(memory_space=pl.ANY),
            pl.BlockSpec((d, tc), lambda s: (0, s % nj)),
            pl.BlockSpec((d, tc), lambda s: (0, nj + s % nj)),
            whole(conv_w_chunks), whole(conv_b_chunks),
            pl.BlockSpec((tc, d), lambda s: (j2(s), 0)),
            whole(ln_g), whole(ln_b),
        ],
        out_specs=pl.BlockSpec(memory_space=pl.ANY),
        out_shape=jax.ShapeDtypeStruct((t, d), _F32),
        scratch_shapes=[u_buf, u_buf, act_buf, u_buf, u_buf, act_buf,
                        pltpu.VMEM((2, tm, d), _F32),
                        pltpu.VMEM((nj, SUBLANES, tc), _F32),
                        pltpu.VMEM((nj, SUBLANES, tc), _F32),
                        pltpu.SemaphoreType.DMA((2,)), pltpu.SemaphoreType.DMA((2,))],
        compiler_params=_params(1),
        name="conv_ffn",
    )(hb, h, w_up_b, w_up_b, conv_w_chunks, conv_b_chunks, w_down_b, ln_g, ln_b)


def kernel(x, w_in, conv_mix_w, w_out, ln1_g, ln1_b, w_up, ffn_conv_w, ffn_conv_b, w_down, ln2_g, ln2_b, rel_bias):
    batch, seq, d_model = x.shape
    depth = w_in.shape[0]
    conv_width = conv_mix_w.shape[-1]
    qkv_width = (w_in.shape[-1] - 3 * conv_width) // 3
    gw = qkv_width // N_DIL
    n_heads = gw // HEAD_DIM
    alpha = (2 * depth) ** 0.25
    h = x.reshape(batch * seq, d_model)
    for layer in range(depth):
        w_in_b = w_in[layer].astype(_BF16)
        xs = _x_prep(h, batch, seq)
        y_conv = _conv_mix(xs[1], w_in_b, conv_mix_w[layer], seq, conv_width)
        outs, lses = [], []
        for g, (window, dilation) in enumerate(DIL_PAIRS):
            qkv_g = _qkv_proj(xs[dilation], w_in_b, seq, 3 * conv_width, qkv_width, g, gw)
            o_g, lse_g = _attention_group(qkv_g, rel_bias[:, g * n_heads:(g + 1) * n_heads], batch, seq,
                                          window, dilation)
            outs.append(o_g)
            lses.append(lse_g)
        h, hb = _mix_out(h, y_conv, outs, lses, w_out[layer].astype(_BF16),
                         ln1_g[layer][None], ln1_b[layer][None], seq, alpha)
        h = _conv_ffn(h, hb, w_up[layer].astype(_BF16), ffn_conv_w[layer], ffn_conv_b[layer][None],
                      w_down[layer].astype(_BF16), ln2_g[layer][None], ln2_b[layer][None], seq, alpha)
    return h.reshape(batch, seq, d_model)
```
